```python
import math
import jax, jax.numpy as jnp
from jax import lax
import numpy as np

D_MODEL = 1024
BATCH = 8
SEQ = 4096
DEPTH = 1

GLA_HEADS = 4
GLA_DK = D_MODEL // 2
GLA_DV = D_MODEL
GLA_GATE_RANK = 16
GLA_TAU = 16.0
GLA_CHUNK = 64
DSA_HEADS = 8
DSA_HEAD_DIM = 128
DSA_LATENT = 256
IDX_HEADS = 8
IDX_DIM = 64
DSA_TOPK = 256
Q_BLOCK = 128
PEER_HEADS = 8
PEER_NKEYS = 128
PEER_NEXPERTS = PEER_NKEYS * PEER_NKEYS
PEER_DKEY = 256
PEER_TOPK = 16
PEER_TOKEN_BLOCK = 128
NORM_EPS = 1e-6

IN_SIZES = (GLA_DK, GLA_DK, GLA_DV, GLA_DV, GLA_GATE_RANK,
            DSA_HEADS * DSA_HEAD_DIM, DSA_LATENT, IDX_HEADS * IDX_DIM, IDX_DIM, IDX_HEADS,
            D_MODEL, D_MODEL)
IN_TOTAL = sum(IN_SIZES)

kernel_name = "hybrid_gla_dsa_peer_block"


def _rmsnorm(x, g):
    x32 = x.astype(jnp.float32)
    y = x32 * lax.rsqrt(jnp.mean(x32 * x32, axis=-1, keepdims=True) + NORM_EPS)
    return (y * g.astype(jnp.float32)).astype(x.dtype)


def _alibi_slopes(n_heads):
    return jnp.exp2(-8.0 * jnp.arange(1, n_heads + 1, dtype=jnp.float32) / n_heads)


def _gla_chunked(q, k, v, log_alpha):
    B, H, S, dk = q.shape
    dv = v.shape[-1]
    C = GLA_CHUNK
    n = S // C
    q, k, v, la = [t.astype(jnp.float32).reshape(B, H, n, C, t.shape[-1]) for t in (q, k, v, log_alpha)]
    b = jnp.cumsum(la, axis=3)
    b_last = b[:, :, :, -1:, :]
    q_dec = q * jnp.exp(b)
    k_inv = k * jnp.exp(-b)
    k_end = k * jnp.exp(b_last - b)
    causal = jnp.tril(jnp.ones((C, C), dtype=bool))
    attn = jnp.where(causal, jnp.einsum('bhnid,bhnjd->bhnij', q_dec, k_inv), 0.0)
    o_intra = jnp.einsum('bhnij,bhnjv->bhniv', attn, v)

    def step(state, inp):
        q_c, k_c, v_c, decay_c = inp
        o_c = jnp.einsum('bhid,bhdv->bhiv', q_c, state)
        state = decay_c[..., None] * state + jnp.einsum('bhjd,bhjv->bhdv', k_c, v_c)
        return state, o_c

    xs = (jnp.moveaxis(q_dec, 2, 0), jnp.moveaxis(k_end, 2, 0), jnp.moveaxis(v, 2, 0),
          jnp.moveaxis(jnp.exp(b_last[:, :, :, 0, :]), 2, 0))
    state0 = jnp.zeros((B, H, dk, dv), jnp.float32)
    _, o_inter = lax.scan(step, state0, xs)
    o = o_intra + jnp.moveaxis(o_inter, 0, 2)
    return o.reshape(B, H, S, dv)


def _gla_branch(q, k, v, r, g_low, w_gate_up, b_gate, norm_g):
    B, S, _ = q.shape
    dkh = GLA_DK // GLA_HEADS
    dvh = GLA_DV // GLA_HEADS
    gate_logit = (g_low @ w_gate_up + b_gate).astype(jnp.float32)
    log_alpha = jax.nn.log_sigmoid(gate_logit) / GLA_TAU

    def heads(t, d):
        return t.reshape(B, S, GLA_HEADS, d).transpose(0, 2, 1, 3)

    o = _gla_chunked(heads(q, dkh) * (dkh ** -0.5), heads(k, dkh), heads(v, dvh), heads(log_alpha, dkh))
    o = o.transpose(0, 2, 1, 3)
    o = _rmsnorm(o, norm_g.reshape(GLA_HEADS, dvh)).reshape(B, S, GLA_DV)
    return o.astype(r.dtype) * jax.nn.silu(r)


def _dsa_branch(q, kv_raw, iq, ik, iw, kv_norm_g, w_uk, w_uv):
    B, S, _ = q.shape
    n_blocks = S // Q_BLOCK
    k_sel = min(DSA_TOPK, S // 4)
    c_kv = _rmsnorm(kv_raw, kv_norm_g)
    slopes = _alibi_slopes(DSA_HEADS)
    key_pos = jnp.arange(S, dtype=jnp.int32)

    def to_blocks(t):
        return t.reshape((B, n_blocks, Q_BLOCK) + t.shape[2:]).swapaxes(0, 1)

    q_b = to_blocks(q.reshape(B, S, DSA_HEADS, DSA_HEAD_DIM))
    iq_b = to_blocks(iq.reshape(B, S, IDX_HEADS, IDX_DIM))
    iw_b = to_blocks(iw)

    def block(inp):
        qb, iqb, iwb, blk = inp
        t = blk * Q_BLOCK + jnp.arange(Q_BLOCK, dtype=jnp.int32)
        rel = jax.nn.relu(jnp.einsum('bqhd,bsd->bqhs', iqb, ik).astype(jnp.float32) * (IDX_DIM ** -0.5))
        score = jnp.einsum('bqh,bqhs->bqs', iwb.astype(jnp.float32) * (IDX_HEADS ** -0.5), rel)
        score = jnp.where(key_pos[None, None, :] <= t[None, :, None], score, -jnp.inf)
        _, idx = lax.top_k(score, k_sel)
        kv_sel = jax.vmap(lambda cb, ib: cb[ib])(c_kv, idx)
        q_lat = jnp.einsum('bqhd,hdc->bqhc', qb, w_uk)
        logits = jnp.einsum('bqhc,bqkc->bqhk', q_lat, kv_sel).astype(jnp.float32) * (DSA_HEAD_DIM ** -0.5)
        dist = t[None, :, None] - idx
        logits = logits - slopes[None, None, :, None] * dist[:, :, None, :].astype(jnp.float32)
        logits = jnp.where((dist >= 0)[:, :, None, :], logits, -jnp.inf)
        p = jax.nn.softmax(logits, axis=-1).astype(kv_sel.dtype)
        o_lat = jnp.einsum('bqhk,bqkc->bqhc', p, kv_sel)
        o = jnp.einsum('bqhc,hcd->bqhd', o_lat, w_uv)
        return o.reshape(B, Q_BLOCK, DSA_HEADS * DSA_HEAD_DIM)

    out = lax.map(block, (q_b, iq_b, iw_b, jnp.arange(n_blocks, dtype=jnp.int32)))
    return out.swapaxes(0, 1).reshape(B, S, DSA_HEADS * DSA_HEAD_DIM)


def _peer(h, w_q, sub_keys_1, sub_keys_2, expert_u, expert_v):
    B, S, D = h.shape
    half = PEER_DKEY // 2
    q = (h @ w_q).reshape(B, S, PEER_HEADS, PEER_DKEY)
    s1 = jnp.einsum('bshd,nd->bshn', q[..., :half], sub_keys_1).astype(jnp.float32)
    s2 = jnp.einsum('bshd,nd->bshn', q[..., half:], sub_keys_2).astype(jnp.float32)
    v1, i1 = lax.top_k(s1, PEER_TOPK)
    v2, i2 = lax.top_k(s2, PEER_TOPK)
    cand = (v1[..., :, None] + v2[..., None, :]).reshape(B, S, PEER_HEADS, PEER_TOPK * PEER_TOPK)
    best, ci = lax.top_k(cand, PEER_TOPK)
    ia = jnp.take_along_axis(i1, ci // PEER_TOPK, axis=-1)
    ib = jnp.take_along_axis(i2, ci % PEER_TOPK, axis=-1)
    expert_idx = ia * PEER_NKEYS + ib
    gates = jax.nn.softmax(best, axis=-1)
    n_blocks = (B * S) // PEER_TOKEN_BLOCK
    h_b = h.reshape(n_blocks, PEER_TOKEN_BLOCK, D)
    i_b = expert_idx.reshape(n_blocks, PEER_TOKEN_BLOCK, PEER_HEADS, PEER_TOPK)
    g_b = gates.reshape(n_blocks, PEER_TOKEN_BLOCK, PEER_HEADS, PEER_TOPK)

    def block(inp):
        hx, ix, gx = inp
        u = expert_u[ix]
        v = expert_v[ix]
        a = jax.nn.gelu(jnp.einsum('td,thkd->thk', hx, u).astype(jnp.float32), approximate=False) * gx
        return jnp.einsum('thk,thkd->td', a.astype(hx.dtype), v)

    return lax.map(block, (h_b, i_b, g_b)).reshape(B, S, D)


def setup_inputs(seed: int = 0) -> dict:
    key = jax.random.key(seed)
    ks = jax.random.split(key, 24)
    D = D_MODEL

    def nrm(k, shape, scale):
        return jax.random.normal(k, shape, jnp.float32) * scale

    def gain(k, shape):
        return 1.0 + 0.1 * jax.random.normal(k, shape, jnp.float32)

    return {
        "x": nrm(ks[0], (BATCH, SEQ, D), 1.0),
        "c": nrm(ks[1], (BATCH, D), 1.0),
        "w_ada": nrm(ks[2], (DEPTH, D, 6 * D), 0.5 * D ** -0.5),
        "b_ada": nrm(ks[3], (DEPTH, 6 * D), 0.02),
        "norm1_g": gain(ks[4], (DEPTH, D)),
        "w_in": nrm(ks[5], (DEPTH, D, IN_TOTAL), D ** -0.5),
        "gla_w_gate_up": nrm(ks[6], (DEPTH, GLA_GATE_RANK, GLA_DK), GLA_GATE_RANK ** -0.5),
        "gla_b_gate": nrm(ks[7], (DEPTH, GLA_DK), 0.1),
        "gla_norm_g": gain(ks[8], (DEPTH, GLA_DV)),
        "dsa_kv_norm_g": gain(ks[9], (DEPTH, DSA_LATENT)),
        "dsa_w_uk": nrm(ks[10], (DEPTH, DSA_HEADS, DSA_HEAD_DIM, DSA_LATENT), DSA_HEAD_DIM ** -0.5),
        "dsa_w_uv": nrm(ks[11], (DEPTH, DSA_HEADS, DSA_LATENT, DSA_HEAD_DIM), DSA_LATENT ** -0.5),
        "w_branch_a": nrm(ks[12], (DEPTH, GLA_DV, D), GLA_DV ** -0.5),
        "w_branch_b": nrm(ks[13], (DEPTH, DSA_HEADS * DSA_HEAD_DIM, D), (DSA_HEADS * DSA_HEAD_DIM) ** -0.5),
        "w_out": nrm(ks[14], (DEPTH, D, D), D ** -0.5),
        "norm2_g": gain(ks[15], (DEPTH, D)),
        "peer_w_q": nrm(ks[16], (DEPTH, D, PEER_HEADS * PEER_DKEY), D ** -0.5),
        "peer_sub_keys_1": nrm(ks[17], (DEPTH, PEER_NKEYS, PEER_DKEY // 2), (PEER_DKEY // 2) ** -0.5),
        "peer_sub_keys_2": nrm(ks[18], (DEPTH, PEER_NKEYS, PEER_DKEY // 2), (PEER_DKEY // 2) ** -0.5),
        "peer_u": nrm(ks[19], (DEPTH, PEER_NEXPERTS, D), D ** -0.5),
        "peer_v": nrm(ks[20], (DEPTH, PEER_NEXPERTS, D), PEER_HEADS ** -0.5),
        "final_norm_g": gain(ks[21], (D,)),
    }


def reference(x, c, w_ada, b_ada, norm1_g, w_in, gla_w_gate_up, gla_b_gate, gla_norm_g,
              dsa_kv_norm_g, dsa_w_uk, dsa_w_uv, w_branch_a, w_branch_b, w_out, norm2_g,
              peer_w_q, peer_sub_keys_1, peer_sub_keys_2, peer_u, peer_v, final_norm_g):
    split_points = np.cumsum(IN_SIZES)[:-1].tolist()
    for l in range(DEPTH):
        ada = jax.nn.silu(c) @ w_ada[l] + b_ada[l]
        shift1, scale1, gate1, shift2, scale2, gate2 = jnp.split(ada[:, None, :], 6, axis=-1)

        h = _rmsnorm(x, norm1_g[l]) * (1.0 + scale1) + shift1
        proj = h @ w_in[l]
        (gq, gk, gv, gr, glow, dq, dkv, iq, ik, iw, za, zb) = jnp.split(proj, split_points, axis=-1)
        ya = _gla_branch(gq, gk, gv, gr, glow, gla_w_gate_up[l], gla_b_gate[l], gla_norm_g[l]) @ w_branch_a[l]
        yb = _dsa_branch(dq, dkv, iq, ik, iw, dsa_kv_norm_g[l], dsa_w_uk[l], dsa_w_uv[l]) @ w_branch_b[l]
        y = (jax.nn.sigmoid(za) * ya + jax.nn.sigmoid(zb) * yb) @ w_out[l]
        x = x + gate1 * y

        h2 = _rmsnorm(x, norm2_g[l]) * (1.0 + scale2) + shift2
        x = x + gate2 * _peer(h2, peer_w_q[l], peer_sub_keys_1[l], peer_sub_keys_2[l], peer_u[l], peer_v[l])
    return _rmsnorm(x, final_norm_g)
```

```python
import functools

import jax
import jax.numpy as jnp
from jax import lax
from jax.experimental import pallas as pl
from jax.experimental.pallas import tpu as pltpu

F32 = jnp.float32
BF16 = jnp.bfloat16
I32 = jnp.int32

D_MODEL = 1024
GLA_HEADS = 4
GLA_DK = D_MODEL // 2
GLA_DV = D_MODEL
GLA_GATE_RANK = 16
GLA_TAU = 16.0
GLA_CHUNK = 64
DSA_HEADS = 8
DSA_HEAD_DIM = 128
DSA_LATENT = 256
IDX_HEADS = 8
IDX_DIM = 64
DSA_TOPK = 256
Q_BLOCK = 128
PEER_HEADS = 8
PEER_NKEYS = 128
PEER_DKEY = 256
PEER_TOPK = 16
NORM_EPS = 1e-6

INT_MIN = -(2 ** 31)
MASK_NEG = -1e30

COL_GV, COL_GR, COL_DQ, COL_ZA, COL_ZB = 0, 1024, 2048, 3072, 4096
COL_GQ, COL_GK, COL_IQ, COL_DKV, COL_MISC = 5120, 5632, 6144, 6656, 6912
PROJ_COLS = 7168
MISC_GLOW, MISC_IK, MISC_IW = 0, 16, 80

VMEM_LIMIT = 56 * 1024 * 1024


def _params(sem):
    return pltpu.CompilerParams(dimension_semantics=sem, vmem_limit_bytes=VMEM_LIMIT)


def _dot(a, b):
    return jnp.dot(a, b, preferred_element_type=F32)


def _dot_nt(a, b):
    return lax.dot_general(a, b, (((1,), (1,)), ((), ())), preferred_element_type=F32)


def _dot_tn(a, b):
    return lax.dot_general(a, b, (((0,), (0,)), ((), ())), preferred_element_type=F32)


def _ada_kernel(c_ref, w_ref, b_ref, o_ref):
    c = c_ref[...]
    a = c * jax.nn.sigmoid(c)
    o_ref[...] = _dot(a.astype(BF16), w_ref[...].astype(BF16)) + b_ref[...]


def _ada(c, w_ada, b_ada):
    B, D = c.shape
    N = w_ada.shape[1]
    tn = 1024
    return pl.pallas_call(
        _ada_kernel,
        grid=(N // tn,),
        in_specs=[pl.BlockSpec((B, D), lambda j: (0, 0)),
                  pl.BlockSpec((D, tn), lambda j: (0, j)),
                  pl.BlockSpec((1, tn), lambda j: (0, j))],
        out_specs=pl.BlockSpec((B, tn), lambda j: (0, j)),
        out_shape=jax.ShapeDtypeStruct((B, N), F32),
        compiler_params=_params(("parallel",)),
        name="ada",
    )(c, w_ada, b_ada.reshape(1, N))


def _proj_kernel(x_ref, sc_ref, sh_ref, g_ref, w_ref, o_ref, h_ref):
    @pl.when(pl.program_id(1) == 0)
    def _():
        x = x_ref[...]
        ms = jnp.mean(x * x, axis=-1, keepdims=True)
        y = x * lax.rsqrt(ms + NORM_EPS) * g_ref[...]
        h_ref[...] = (y * (1.0 + sc_ref[0]) + sh_ref[0]).astype(BF16)

    o_ref[...] = _dot(h_ref[...], w_ref[...])


def _proj(x2, ada3, norm_g, w_cat, S):
    M, D = x2.shape
    N = w_cat.shape[1]
    tm, tn = min(1024, S), 512
    tpb = S // tm
    return pl.pallas_call(
        _proj_kernel,
        grid=(M // tm, N // tn),
        in_specs=[pl.BlockSpec((tm, D), lambda i, j: (i, 0)),
                  pl.BlockSpec((1, 1, D), lambda i, j: (i // tpb, 0, 1)),
                  pl.BlockSpec((1, 1, D), lambda i, j: (i // tpb, 0, 0)),
                  pl.BlockSpec((1, D), lambda i, j: (0, 0)),
                  pl.BlockSpec((D, tn), lambda i, j: (0, j))],
        out_specs=pl.BlockSpec((tm, tn), lambda i, j: (i, j)),
        out_shape=jax.ShapeDtypeStruct((M, N), F32),
        scratch_shapes=[pltpu.VMEM((tm, D), BF16)],
        compiler_params=_params(("parallel", "arbitrary")),
        name="proj",
    )(x2, ada3, ada3, norm_g.reshape(1, D), w_cat)


def _gla_kernel(q_ref, k_ref, v_ref, r_ref, misc_ref, wg_ref, bg_ref, ng_ref, o_ref, st_ref, *, n_chunks):
    C = GLA_CHUNK
    dkh = GLA_DK // GLA_HEADS

    @pl.when(pl.program_id(2) == 0)
    def _():
        st_ref[...] = jnp.zeros_like(st_ref)

    wg = wg_ref[...].astype(BF16)
    bg = bg_ref[...]
    ng = ng_ref[...]
    row = lax.broadcasted_iota(I32, (C, C), 0)
    col = lax.broadcasted_iota(I32, (C, C), 1)
    causal = row >= col
    tril = jnp.where(causal, 1.0, 0.0).astype(BF16)

    st = st_ref[...]
    for c in range(n_chunks):
        sl = slice(c * C, (c + 1) * C)
        glow = misc_ref[sl, MISC_GLOW:MISC_GLOW + GLA_GATE_RANK]
        gate = _dot(glow.astype(BF16), wg) + bg
        la = (jnp.minimum(gate, 0.0) - jnp.log1p(jnp.exp(-jnp.abs(gate)))) * (1.0 / GLA_TAU)
        la_hi = la.astype(BF16)
        la_lo = (la - la_hi.astype(F32)).astype(BF16)
        b = _dot(tril, la_hi) + _dot(tril, la_lo)
        b_last = b[C - 1:C, :]
        q_dec = (q_ref[sl, :] * (dkh ** -0.5) * jnp.exp(b)).astype(BF16)
        k = k_ref[sl, :]
        k_inv = (k * jnp.exp(-b)).astype(BF16)
        k_end = (k * jnp.exp(b_last - b)).astype(BF16)
        v = v_ref[sl, :].astype(BF16)
        attn = jnp.where(causal, _dot_nt(q_dec, k_inv), 0.0)
        o = _dot(attn.astype(BF16), v) + _dot_nt(q_dec, st.astype(BF16))
        st = jnp.exp(b_last) * st + _dot_tn(v, k_end)
        ms = jnp.mean(o * o, axis=-1, keepdims=True)
        on = o * lax.rsqrt(ms + NORM_EPS) * ng
        r = r_ref[sl, :]
        o_ref[sl, :] = (on * (r * jax.nn.sigmoid(r))).astype(BF16)
    st_ref[...] = st


def _gla(proj, w_gate_up, b_gate, norm_g, B, S):
    M = proj.shape[0]
    rb = min(512, S)
    nrb = S // rb
    dkh = GLA_DK // GLA_HEADS
    dvh = GLA_DV // GLA_HEADS
    rowmap = lambda b, h, c: b * nrb + c
    return pl.pallas_call(
        functools.partial(_gla_kernel, n_chunks=rb // GLA_CHUNK),
        grid=(B, GLA_HEADS, nrb),
        in_specs=[pl.BlockSpec((rb, dkh), lambda b, h, c: (rowmap(b, h, c), COL_GQ // dkh + h)),
                  pl.BlockSpec((rb, dkh), lambda b, h, c: (rowmap(b, h, c), COL_GK // dkh + h)),
                  pl.BlockSpec((rb, dvh), lambda b, h, c: (rowmap(b, h, c), COL_GV // dvh + h)),
                  pl.BlockSpec((rb, dvh), lambda b, h, c: (rowmap(b, h, c), COL_GR // dvh + h)),
                  pl.BlockSpec((rb, 128), lambda b, h, c: (rowmap(b, h, c), COL_MISC // 128)),
                  pl.BlockSpec((GLA_GATE_RANK, dkh), lambda b, h, c: (0, h)),
                  pl.BlockSpec((1, dkh), lambda b, h, c: (0, h)),
                  pl.BlockSpec((1, dvh), lambda b, h, c: (0, h))],
        out_specs=pl.BlockSpec((rb, dvh), lambda b, h, c: (rowmap(b, h, c), h)),
        out_shape=jax.ShapeDtypeStruct((M, GLA_DV), BF16),
        scratch_shapes=[pltpu.VMEM((dvh, dkh), F32)],
        compiler_params=_params(("parallel", "parallel", "arbitrary")),
        name="gla",
    )(proj, proj, proj, proj, proj, w_gate_up, b_gate.reshape(1, GLA_DK), norm_g.reshape(1, GLA_DV))


def _kvprep_kernel(kv_ref, misc_ref, g_ref, ckv_ref, ik_ref):
    x = kv_ref[...]
    ms = jnp.mean(x * x, axis=-1, keepdims=True)
    ckv_ref[...] = (x * lax.rsqrt(ms + NORM_EPS) * g_ref[...]).astype(BF16)
    ik_ref[...] = misc_ref[:, MISC_IK:MISC_IK + IDX_DIM].astype(BF16)


def _kvprep(proj, kv_norm_g):
    M = proj.shape[0]
    tm = 1024 if M % 1024 == 0 else 512
    return pl.pallas_call(
        _kvprep_kernel,
        grid=(M // tm,),
        in_specs=[pl.BlockSpec((tm, DSA_LATENT), lambda i: (i, COL_DKV // DSA_LATENT)),
                  pl.BlockSpec((tm, 128), lambda i: (i, COL_MISC // 128)),
                  pl.BlockSpec((1, DSA_LATENT), lambda i: (0, 0))],
        out_specs=[pl.BlockSpec((tm, DSA_LATENT), lambda i: (i, 0)),
                   pl.BlockSpec((tm, IDX_DIM), lambda i: (i, 0))],
        out_shape=[jax.ShapeDtypeStruct((M, DSA_LATENT), BF16),
                   jax.ShapeDtypeStruct((M, IDX_DIM), BF16)],
        compiler_params=_params(("parallel",)),
        name="kvprep",
    )(proj, proj, kv_norm_g.reshape(1, DSA_LATENT))


def _dsa_kernel(dq_ref, iq_ref, misc_ref, ik_ref, ckv_ref, wuk_ref, wuv_ref, o_ref,
                keys_ref, qlat_ref, m_ref, l_ref, acc_ref, thr_ref, tie_ref, *, S, KC, k_sel):
    QB = Q_BLOCK
    H = DSA_HEADS
    qb = pl.program_id(1)
    nkc = (qb * QB + QB + KC - 1) // KC
    t_pos = qb * QB + lax.broadcasted_iota(I32, (QB, 1), 0)
    lane_pos = lax.broadcasted_iota(I32, (1, KC), 1)

    iw = misc_ref[:, MISC_IW:MISC_IW + IDX_HEADS] * (IDX_HEADS ** -0.5)
    iq = iq_ref[...].astype(BF16)

    def score_body(kc, carry):
        k0 = pl.multiple_of(kc * KC, KC)
        ikc = ik_ref[pl.ds(k0, KC), :]
        score = jnp.zeros((QB, KC), F32)
        for h in range(IDX_HEADS):
            rel = _dot_nt(iq[:, h * IDX_DIM:(h + 1) * IDX_DIM], ikc)
            rel = jnp.maximum(rel * (IDX_DIM ** -0.5), 0.0)
            score = score + iw[:, h:h + 1] * rel
        bits = pltpu.bitcast(score, I32)
        keyv = bits ^ ((bits >> 31) & 0x7FFFFFFF)
        valid = (k0 + lane_pos) <= t_pos
        keys_ref[kc] = jnp.where(valid, keyv, INT_MIN)
        return carry

    lax.fori_loop(0, nkc, score_body, 0)

    def count(pred):
        def body(kc, acc):
            kv = keys_ref[kc]
            for g in range(KC // 128):
                acc = acc + jnp.where(pred(kv[:, g * 128:(g + 1) * 128], kc * KC + g * 128), 1.0, 0.0)
            return acc
        acc = lax.fori_loop(0, nkc, body, jnp.zeros((QB, 128), F32))
        return jnp.sum(acc, axis=1, keepdims=True)

    def bit_body(i, thr):
        cand = thr + (jnp.int32(1) << (31 - i))
        cnt = count(lambda kv, base: kv >= cand)
        return jnp.where(cnt >= k_sel, cand, thr)

    thr = lax.fori_loop(0, 32, bit_body, jnp.full((QB, 1), INT_MIN, I32))
    thr_ref[...] = thr
    n_gt = count(lambda kv, base: kv > thr)
    n_ge = count(lambda kv, base: kv >= thr)
    tie_ref[...] = jnp.full((QB, 1), S, I32)

    @pl.when(jnp.max(n_ge) > k_sel)
    def _():
        need = k_sel - n_gt
        lane = lax.broadcasted_iota(I32, (1, 128), 1)

        def pos_body(i, p):
            cand = p + (jnp.int32(1) << (S.bit_length() - 1 - i))
            cnt = count(lambda kv, base: (kv == thr) & ((base + lane) < cand))
            return jnp.where(cnt < need, cand, p)

        tie_ref[...] = lax.fori_loop(0, S.bit_length(), pos_body, jnp.zeros((QB, 1), I32))

    thr = thr_ref[...]
    tie = tie_ref[...]

    for h in range(H):
        ql = _dot(dq_ref[:, h * DSA_HEAD_DIM:(h + 1) * DSA_HEAD_DIM].astype(BF16), wuk_ref[h])
        qlat_ref[h * QB:(h + 1) * QB, :] = ql.astype(BF16)
    m_ref[...] = jnp.full(m_ref.shape, MASK_NEG, F32)
    l_ref[...] = jnp.zeros(l_ref.shape, F32)
    acc_ref[...] = jnp.zeros(acc_ref.shape, F32)

    def att_body(kc, carry):
        k0 = pl.multiple_of(kc * KC, KC)
        kv = ckv_ref[pl.ds(k0, KC), :]
        keyv = keys_ref[kc]
        kpos = k0 + lane_pos
        dist = t_pos - kpos
        sel = ((keyv > thr) | ((keyv == thr) & (kpos <= tie))) & (dist >= 0)
        distf = dist.astype(F32)
        logits = _dot_nt(qlat_ref[...], kv) * (DSA_HEAD_DIM ** -0.5)
        for h in range(H):
            rows = slice(h * QB, (h + 1) * QB)
            slope = 2.0 ** (-8.0 * (h + 1) / H)
            lg = jnp.where(sel, logits[rows] - slope * distf, MASK_NEG)
            m_old = m_ref[rows]
            m_new = jnp.maximum(m_old, jnp.max(lg, axis=1, keepdims=True))
            alpha = jnp.exp(m_old - m_new)
            p = jnp.exp(lg - m_new)
            l_ref[rows] = alpha * l_ref[rows] + jnp.sum(p, axis=1, keepdims=True)
            acc_ref[rows] = alpha * acc_ref[rows] + _dot(p.astype(BF16), kv)
            m_ref[rows] = m_new
        return carry

    lax.fori_loop(0, nkc, att_body, 0)

    for h in range(H):
        rows = slice(h * QB, (h + 1) * QB)
        o_lat = acc_ref[rows] / l_ref[rows]
        o_ref[:, h * DSA_HEAD_DIM:(h + 1) * DSA_HEAD_DIM] = _dot(o_lat.astype(BF16), wuv_ref[h]).astype(BF16)


def _dsa(proj, ik, ckv, w_uk, w_uv, B, S):
    M = proj.shape[0]
    QB = Q_BLOCK
    nqb = S // QB
    KC = min(512, S)
    k_sel = min(DSA_TOPK, S // 4)
    HD = DSA_HEADS * DSA_HEAD_DIM
    return pl.pallas_call(
        functools.partial(_dsa_kernel, S=S, KC=KC, k_sel=k_sel),
        grid=(B, nqb),
        in_specs=[pl.BlockSpec((QB, HD), lambda b, q: (b * nqb + q, COL_DQ // HD)),
                  pl.BlockSpec((QB, IDX_HEADS * IDX_DIM), lambda b, q: (b * nqb + q, COL_IQ // (IDX_HEADS * IDX_DIM))),
                  pl.BlockSpec((QB, 128), lambda b, q: (b * nqb + q, COL_MISC // 128)),
                  pl.BlockSpec((S, IDX_DIM), lambda b, q: (b, 0)),
                  pl.BlockSpec((S, DSA_LATENT), lambda b, q: (b, 0)),
                  pl.BlockSpec((DSA_HEADS, DSA_HEAD_DIM, DSA_LATENT), lambda b, q: (0, 0, 0)),
                  pl.BlockSpec((DSA_HEADS, DSA_LATENT, DSA_HEAD_DIM), lambda b, q: (0, 0, 0))],
        out_specs=pl.BlockSpec((QB, HD), lambda b, q: (b * nqb + q, 0)),
        out_shape=jax.ShapeDtypeStruct((M, HD), BF16),
        scratch_shapes=[pltpu.VMEM((S // KC, QB, KC), I32),
                        pltpu.VMEM((DSA_HEADS * QB, DSA_LATENT), BF16),
                        pltpu.VMEM((DSA_HEADS * QB, 1), F32),
                        pltpu.VMEM((DSA_HEADS * QB, 1), F32),
                        pltpu.VMEM((DSA_HEADS * QB, DSA_LATENT), F32),
                        pltpu.VMEM((QB, 1), I32),
                        pltpu.VMEM((QB, 1), I32)],
        compiler_params=_params(("parallel", "arbitrary")),
        name="dsa",
    )(proj, proj, proj, ik, ckv, w_uk, w_uv)


def _merge_kernel(ya_ref, yb_ref, za_ref, zb_ref, x_ref, g1_ref, sc2_ref, sh2_ref, n2_ref,
                  wa_ref, wb_ref, wo_ref, wq_ref, k1_ref, k2_ref,
                  x1_ref, h2_ref, s1_ref, s2_ref):
    ya = _dot(ya_ref[...], wa_ref[...])
    yb = _dot(yb_ref[...], wb_ref[...])
    mix = jax.nn.sigmoid(za_ref[...]) * ya + jax.nn.sigmoid(zb_ref[...]) * yb
    y = _dot(mix.astype(BF16), wo_ref[...])
    x1 = x_ref[...] + g1_ref[0] * y
    x1_ref[...] = x1
    ms = jnp.mean(x1 * x1, axis=-1, keepdims=True)
    h2 = (x1 * lax.rsqrt(ms + NORM_EPS) * n2_ref[...]) * (1.0 + sc2_ref[0]) + sh2_ref[0]
    h2b = h2.astype(BF16)
    h2_ref[...] = h2b
    q = _dot(h2b, wq_ref[...]).astype(BF16)
    half = PEER_DKEY // 2
    for h in range(PEER_HEADS):
        s1_ref[h] = _dot_nt(k1_ref[...], q[:, h * PEER_DKEY:h * PEER_DKEY + half])
        s2_ref[h] = _dot_nt(k2_ref[...], q[:, h * PEER_DKEY + half:(h + 1) * PEER_DKEY])


def _merge(ya, yb, proj, x2, ada3, norm2_g, wa, wb, wo, wq, k1, k2, S):
    M, D = x2.shape
    tm = min(512, S)
    tpb = S // tm
    full = lambda shape: pl.BlockSpec(shape, lambda i: (0,) * len(shape))
    adaspec = lambda k: pl.BlockSpec((1, 1, D), lambda i: (i // tpb, 0, k))
    return pl.pallas_call(
        _merge_kernel,
        grid=(M // tm,),
        in_specs=[pl.BlockSpec((tm, D), lambda i: (i, 0)),
                  pl.BlockSpec((tm, D), lambda i: (i, 0)),
                  pl.BlockSpec((tm, D), lambda i: (i, COL_ZA // D)),
                  pl.BlockSpec((tm, D), lambda i: (i, COL_ZB // D)),
                  pl.BlockSpec((tm, D), lambda i: (i, 0)),
                  adaspec(2), adaspec(4), adaspec(3),
                  full((1, D)),
                  full((D, D)), full((D, D)), full((D, D)), full((D, PEER_HEADS * PEER_DKEY)),
                  full((PEER_NKEYS, PEER_DKEY // 2)), full((PEER_NKEYS, PEER_DKEY // 2))],
        out_specs=[pl.BlockSpec((tm, D), lambda i: (i, 0)),
                   pl.BlockSpec((tm, D), lambda i: (i, 0)),
                   pl.BlockSpec((PEER_HEADS, PEER_NKEYS, tm), lambda i: (0, 0, i)),
                   pl.BlockSpec((PEER_HEADS, PEER_NKEYS, tm), lambda i: (0, 0, i))],
        out_shape=[jax.ShapeDtypeStruct((M, D), F32),
                   jax.ShapeDtypeStruct((M, D), BF16),
                   jax.ShapeDtypeStruct((PEER_HEADS, PEER_NKEYS, M), F32),
                   jax.ShapeDtypeStruct((PEER_HEADS, PEER_NKEYS, M), F32)],
        compiler_params=_params(("parallel",)),
        name="merge",
    )(ya, yb, proj, proj, x2, ada3, ada3, ada3, norm2_g.reshape(1, D), wa, wb, wo, wq, k1, k2)


_PAIRS = [(j1, j2) for j1 in range(PEER_TOPK) for j2 in range(PEER_TOPK) if (j1 + 1) * (j2 + 1) <= PEER_TOPK]
_NCAND = -(-len(_PAIRS) // 8) * 8


def _extract_top(s, rounds):
    n = s.shape[0]
    iota = lax.broadcasted_iota(I32, s.shape, 0)
    vals, idxs = [], []
    for _ in range(rounds):
        m = jnp.max(s, axis=0, keepdims=True)
        idx = jnp.min(jnp.where(s == m, iota, n), axis=0, keepdims=True)
        vals.append(m)
        idxs.append(idx)
        s = jnp.where(iota == idx, -jnp.inf, s)
    return vals, idxs


def _route_kernel(s1_ref, s2_ref, ia_ref, ib_ref, g_ref):
    R = s1_ref.shape[2]
    K = PEER_TOPK
    ia_rows, ib_rows, g_rows = [], [], []
    for h in range(PEER_HEADS):
        v1, i1 = _extract_top(s1_ref[h], K)
        v2, i2 = _extract_top(s2_ref[h], K)
        pad = _NCAND - len(_PAIRS)
        cand = jnp.concatenate([v1[a] + v2[b] for a, b in _PAIRS] + [jnp.full((pad, R), -jnp.inf, F32)], axis=0)
        ca = jnp.concatenate([i1[a] for a, b in _PAIRS] + [jnp.zeros((pad, R), I32)], axis=0)
        cb = jnp.concatenate([i2[b] for a, b in _PAIRS] + [jnp.zeros((pad, R), I32)], axis=0)
        best, pos = _extract_top(cand, K)
        iota = lax.broadcasted_iota(I32, cand.shape, 0)
        e = [jnp.exp(bv - best[0]) for bv in best]
        denom = e[0]
        for j in range(1, K):
            denom = denom + e[j]
        for j in range(K):
            hit = iota == pos[j]
            ia_rows.append(jnp.sum(jnp.where(hit, ca, 0), axis=0, keepdims=True).astype(F32))
            ib_rows.append(jnp.sum(jnp.where(hit, cb, 0), axis=0, keepdims=True).astype(F32))
            g_rows.append(e[j] / denom)
    ia_ref[...] = jnp.concatenate(ia_rows, axis=0).T
    ib_ref[...] = jnp.concatenate(ib_rows, axis=0).T
    g_ref[...] = jnp.concatenate(g_rows, axis=0).T


def _route(s1, s2):
    M = s1.shape[2]
    R = 512
    NS = PEER_HEADS * PEER_TOPK
    spec_in = pl.BlockSpec((PEER_HEADS, PEER_NKEYS, R), lambda i: (0, 0, i))
    spec_out = pl.BlockSpec((R, NS), lambda i: (i, 0))
    return pl.pallas_call(
        _route_kernel,
        grid=(M // R,),
        in_specs=[spec_in, spec_in],
        out_specs=[spec_out, spec_out, spec_out],
        out_shape=[jax.ShapeDtypeStruct((M, NS), F32)] * 3,
        compiler_params=_params(("parallel",)),
        name="route",
    )(s1, s2)


def _peer_kernel(h2_ref, ia_ref, ib_ref, g_ref, u_ref, v_ref, x1_ref, g2_ref, fg_ref, o_ref,
                 gm_ref, acc_ref, *, T, AE, final_norm):
    e = pl.program_id(1)
    NK = PEER_NKEYS

    @pl.when(e == 0)
    def _():
        acc_ref[...] = jnp.zeros_like(acc_ref)
        sub = lax.broadcasted_iota(I32, (NK, NK), 0).astype(F32)

        def tok_body(t, carry):
            ia = ia_ref[pl.ds(t, 1), :]
            ib = ib_ref[pl.ds(t, 1), :]
            g = g_ref[pl.ds(t, 1), :]
            am = jnp.where(sub == ia, g, 0.0).astype(BF16)
            bm = jnp.where(sub == ib, 1.0, 0.0).astype(BF16)
            gm_ref[pl.ds(pl.multiple_of(t * NK, NK), NK), :] = _dot_nt(am, bm)
            return carry

        lax.fori_loop(0, T, tok_body, 0)

    s = _dot_nt(h2_ref[...], u_ref[...])
    gt = jnp.concatenate([gm_ref[pl.ds(e * AE + j, T, stride=NK), :] for j in range(AE)], axis=1)
    act = 0.5 * s * (1.0 + lax.erf(s * (0.5 ** 0.5)))
    acc_ref[...] += _dot((act * gt).astype(BF16), v_ref[...])

    @pl.when(e == pl.num_programs(1) - 1)
    def _():
        x2 = x1_ref[...] + g2_ref[0] * acc_ref[...]
        if final_norm:
            ms = jnp.mean(x2 * x2, axis=-1, keepdims=True)
            x2 = x2 * lax.rsqrt(ms + NORM_EPS) * fg_ref[...]
        o_ref[...] = x2


def _peer(h2, ia, ib, g, u, v, x1, ada3, final_g, S, final_norm):
    M, D = x1.shape
    T = min(256, S)
    AE = 8
    ET = AE * PEER_NKEYS
    NE = u.shape[0]
    NS = PEER_HEADS * PEER_TOPK
    tpb = S // T
    return pl.pallas_call(
        functools.partial(_peer_kernel, T=T, AE=AE, final_norm=final_norm),
        grid=(M // T, NE // ET),
        in_specs=[pl.BlockSpec((T, D), lambda i, e: (i, 0)),
                  pl.BlockSpec((T, NS), lambda i, e: (i, 0)),
                  pl.BlockSpec((T, NS), lambda i, e: (i, 0)),
                  pl.BlockSpec((T, NS), lambda i, e: (i, 0)),
                  pl.BlockSpec((ET, D), lambda i, e: (e, 0)),
                  pl.BlockSpec((ET, D), lambda i, e: (e, 0)),
                  pl.BlockSpec((T, D), lambda i, e: (i, 0)),
                  pl.BlockSpec((1, 1, D), lambda i, e: (i // tpb, 0, 5)),
                  pl.BlockSpec((1, D), lambda i, e: (0, 0))],
        out_specs=pl.BlockSpec((T, D), lambda i, e: (i, 0)),
        out_shape=jax.ShapeDtypeStruct((M, D), F32),
        scratch_shapes=[pltpu.VMEM((T * PEER_NKEYS, PEER_NKEYS), F32),
                        pltpu.VMEM((T, D), F32)],
        compiler_params=_params(("parallel", "arbitrary")),
        name="peer",
    )(h2, ia, ib, g, u, v, x1, ada3, final_g.reshape(1, D))


def _cat_w_in(w_in):
    sizes = (GLA_DK, GLA_DK, GLA_DV, GLA_DV, GLA_GATE_RANK, DSA_HEADS * DSA_HEAD_DIM, DSA_LATENT,
             IDX_HEADS * IDX_DIM, IDX_DIM, IDX_HEADS, D_MODEL, D_MODEL)
    offs = [0]
    for s in sizes:
        offs.append(offs[-1] + s)
    gq, gk, gv, gr, glow, dq, dkv, iq, ik, iw, za, zb = [w_in[:, offs[i]:offs[i + 1]] for i in range(len(sizes))]
    pad_misc = jnp.zeros((w_in.shape[0], 128 - (GLA_GATE_RANK + IDX_DIM + IDX_HEADS)), w_in.dtype)
    pad_end = jnp.zeros((w_in.shape[0], PROJ_COLS - (COL_MISC + 128)), w_in.dtype)
    return jnp.concatenate([gv, gr, dq, za, zb, gq, gk, iq, dkv, glow, ik, iw, pad_misc, pad_end], axis=1).astype(BF16)


def kernel(x, c, w_ada, b_ada, norm1_g, w_in, gla_w_gate_up, gla_b_gate, gla_norm_g, dsa_kv_norm_g, dsa_w_uk,
           dsa_w_uv, w_branch_a, w_branch_b, w_out, norm2_g, peer_w_q, peer_sub_keys_1, peer_sub_keys_2,
           peer_u, peer_v, final_norm_g):
    B, S, D = x.shape
    depth = w_ada.shape[0]
    x2 = x.reshape(B * S, D)
    for l in range(depth):
        ada3 = _ada(c, w_ada[l], b_ada[l]).reshape(B, 1, 6 * D)
        proj = _proj(x2, ada3, norm1_g[l], _cat_w_in(w_in[l]), S)
        ya = _gla(proj, gla_w_gate_up[l], gla_b_gate[l], gla_norm_g[l], B, S)
        ckv, ik = _kvprep(proj, dsa_kv_norm_g[l])
        yb = _dsa(proj, ik, ckv, dsa_w_uk[l].astype(BF16), dsa_w_uv[l].astype(BF16), B, S)
        x1, h2, s1, s2 = _merge(ya, yb, proj, x2, ada3, norm2_g[l],
                                w_branch_a[l].astype(BF16), w_branch_b[l].astype(BF16), w_out[l].astype(BF16),
                                peer_w_q[l].astype(BF16), peer_sub_keys_1[l].astype(BF16),
                                peer_sub_keys_2[l].astype(BF16), S)
        ia, ib, g = _route(s1, s2)
        x2 = _peer(h2, ia, ib, g, peer_u[l].astype(BF16), peer_v[l].astype(BF16), x1, ada3, final_norm_g, S,
                   final_norm=(l == depth - 1))
    return x2.reshape(B, S, D)
```

```python
import functools

import jax
import jax.numpy as jnp
from jax import lax
from jax.experimental import pallas as pl
from jax.experimental.pallas import tpu as pltpu

F32 = jnp.float32
BF16 = jnp.bfloat16
I32 = jnp.int32

D_MODEL = 1024
GLA_HEADS = 4
GLA_DK = D_MODEL // 2
GLA_DV = D_MODEL
GLA_GATE_RANK = 16
GLA_TAU = 16.0
GLA_CHUNK = 64
DSA_HEADS = 8
DSA_HEAD_DIM = 128
DSA_LATENT = 256
IDX_HEADS = 8
IDX_DIM = 64
DSA_TOPK = 256
Q_BLOCK = 128
PEER_HEADS = 8
PEER_NKEYS = 128
PEER_DKEY = 256
PEER_TOPK = 16
NORM_EPS = 1e-6

INT_MIN = -(2 ** 31)
MASK_NEG = -1e30
MASK_DIST = 1e30

COL_GV, COL_GR, COL_DQ, COL_ZA, COL_ZB = 0, 1024, 2048, 3072, 4096
COL_GQ, COL_GK, COL_IQ, COL_DKV, COL_MISC = 5120, 5632, 6144, 6656, 6912
PROJ_COLS = 7168
MISC_GLOW, MISC_IK, MISC_IW = 0, 16, 80

VMEM_LIMIT = 56 * 1024 * 1024


def _params(sem):
    return pltpu.CompilerParams(dimension_semantics=sem, vmem_limit_bytes=VMEM_LIMIT)


def _dot(a, b):
    return jnp.dot(a, b, preferred_element_type=F32)


def _dot_nt(a, b):
    return lax.dot_general(a, b, (((1,), (1,)), ((), ())), preferred_element_type=F32)


def _dot_tn(a, b):
    return lax.dot_general(a, b, (((0,), (0,)), ((), ())), preferred_element_type=F32)


def _ada_kernel(c_ref, w_ref, b_ref, o_ref):
    c = c_ref[...]
    a = c * jax.nn.sigmoid(c)
    o_ref[...] = _dot(a.astype(BF16), w_ref[...].astype(BF16)) + b_ref[...]


def _ada(c, w_ada, b_ada):
    B, D = c.shape
    N = w_ada.shape[1]
    tn = 1024
    return pl.pallas_call(
        _ada_kernel,
        grid=(N // tn,),
        in_specs=[pl.BlockSpec((B, D), lambda j: (0, 0)),
                  pl.BlockSpec((D, tn), lambda j: (0, j)),
                  pl.BlockSpec((1, tn), lambda j: (0, j))],
        out_specs=pl.BlockSpec((B, tn), lambda j: (0, j)),
        out_shape=jax.ShapeDtypeStruct((B, N), F32),
        compiler_params=_params(("parallel",)),
        name="ada",
    )(c, w_ada, b_ada.reshape(1, N))


def _proj_kernel(x_ref, sc_ref, sh_ref, g_ref, w_ref, o_ref, h_ref):
    @pl.when(pl.program_id(1) == 0)
    def _():
        x = x_ref[...]
        ms = jnp.mean(x * x, axis=-1, keepdims=True)
        y = x * lax.rsqrt(ms + NORM_EPS) * g_ref[...]
        h_ref[...] = (y * (1.0 + sc_ref[0]) + sh_ref[0]).astype(BF16)

    o_ref[...] = _dot(h_ref[...], w_ref[...])


def _proj(x2, ada3, norm_g, w_cat, S):
    M, D = x2.shape
    N = w_cat.shape[1]
    tm, tn = min(1024, S), 512
    tpb = S // tm
    return pl.pallas_call(
        _proj_kernel,
        grid=(M // tm, N // tn),
        in_specs=[pl.BlockSpec((tm, D), lambda i, j: (i, 0)),
                  pl.BlockSpec((1, 1, D), lambda i, j: (i // tpb, 0, 1)),
                  pl.BlockSpec((1, 1, D), lambda i, j: (i // tpb, 0, 0)),
                  pl.BlockSpec((1, D), lambda i, j: (0, 0)),
                  pl.BlockSpec((D, tn), lambda i, j: (0, j))],
        out_specs=pl.BlockSpec((tm, tn), lambda i, j: (i, j)),
        out_shape=jax.ShapeDtypeStruct((M, N), F32),
        scratch_shapes=[pltpu.VMEM((tm, D), BF16)],
        compiler_params=_params(("parallel", "arbitrary")),
        name="proj",
    )(x2, ada3, ada3, norm_g.reshape(1, D), w_cat)


def _gla_kernel(q_ref, k_ref, v_ref, r_ref, misc_ref, wg_ref, bg_ref, ng_ref, o_ref, st_ref, *, n_chunks):
    C = GLA_CHUNK
    dkh = GLA_DK // GLA_HEADS

    @pl.when(pl.program_id(2) == 0)
    def _():
        st_ref[...] = jnp.zeros_like(st_ref)

    wg = wg_ref[...].astype(BF16)
    bg = bg_ref[...]
    ng = ng_ref[...]
    row = lax.broadcasted_iota(I32, (C, C), 0)
    col = lax.broadcasted_iota(I32, (C, C), 1)
    causal = row >= col
    tril = jnp.where(causal, 1.0, 0.0).astype(BF16)

    st = st_ref[...]
    for c in range(n_chunks):
        sl = slice(c * C, (c + 1) * C)
        glow = misc_ref[sl, MISC_GLOW:MISC_GLOW + GLA_GATE_RANK]
        gate = _dot(glow.astype(BF16), wg) + bg
        la = (jnp.minimum(gate, 0.0) - jnp.log1p(jnp.exp(-jnp.abs(gate)))) * (1.0 / GLA_TAU)
        la_hi = la.astype(BF16)
        la_lo = (la - la_hi.astype(F32)).astype(BF16)
        b = _dot(tril, la_hi) + _dot(tril, la_lo)
        b_last = b[C - 1:C, :]
        q_dec = (q_ref[sl, :] * (dkh ** -0.5) * jnp.exp(b)).astype(BF16)
        k = k_ref[sl, :]
        k_inv = (k * jnp.exp(-b)).astype(BF16)
        k_end = (k * jnp.exp(b_last - b)).astype(BF16)
        v = v_ref[sl, :].astype(BF16)
        attn = jnp.where(causal, _dot_nt(q_dec, k_inv), 0.0)
        o = _dot(attn.astype(BF16), v) + _dot_nt(q_dec, st.astype(BF16))
        st = jnp.exp(b_last) * st + _dot_tn(v, k_end)
        ms = jnp.mean(o * o, axis=-1, keepdims=True)
        on = o * lax.rsqrt(ms + NORM_EPS) * ng
        r = r_ref[sl, :]
        o_ref[sl, :] = (on * (r * jax.nn.sigmoid(r))).astype(BF16)
    st_ref[...] = st


def _gla(proj, w_gate_up, b_gate, norm_g, B, S):
    M = proj.shape[0]
    rb = min(512, S)
    nrb = S // rb
    dkh = GLA_DK // GLA_HEADS
    dvh = GLA_DV // GLA_HEADS
    rowmap = lambda b, h, c: b * nrb + c
    return pl.pallas_call(
        functools.partial(_gla_kernel, n_chunks=rb // GLA_CHUNK),
        grid=(B, GLA_HEADS, nrb),
        in_specs=[pl.BlockSpec((rb, dkh), lambda b, h, c: (rowmap(b, h, c), COL_GQ // dkh + h)),
                  pl.BlockSpec((rb, dkh), lambda b, h, c: (rowmap(b, h, c), COL_GK // dkh + h)),
                  pl.BlockSpec((rb, dvh), lambda b, h, c: (rowmap(b, h, c), COL_GV // dvh + h)),
                  pl.BlockSpec((rb, dvh), lambda b, h, c: (rowmap(b, h, c), COL_GR // dvh + h)),
                  pl.BlockSpec((rb, 128), lambda b, h, c: (rowmap(b, h, c), COL_MISC // 128)),
                  pl.BlockSpec((GLA_GATE_RANK, dkh), lambda b, h, c: (0, h)),
                  pl.BlockSpec((1, dkh), lambda b, h, c: (0, h)),
                  pl.BlockSpec((1, dvh), lambda b, h, c: (0, h))],
        out_specs=pl.BlockSpec((rb, dvh), lambda b, h, c: (rowmap(b, h, c), h)),
        out_shape=jax.ShapeDtypeStruct((M, GLA_DV), BF16),
        scratch_shapes=[pltpu.VMEM((dvh, dkh), F32)],
        compiler_params=_params(("parallel", "parallel", "arbitrary")),
        name="gla",
    )(proj, proj, proj, proj, proj, w_gate_up, b_gate.reshape(1, GLA_DK), norm_g.reshape(1, GLA_DV))


def _kvprep_kernel(kv_ref, misc_ref, g_ref, ckv_ref, ckvt_ref, ikt_ref):
    x = kv_ref[...]
    ms = jnp.mean(x * x, axis=-1, keepdims=True)
    c = x * lax.rsqrt(ms + NORM_EPS) * g_ref[...]
    ckv_ref[...] = c.astype(BF16)
    ckvt_ref[0] = c.T.astype(BF16)
    ikt_ref[0] = misc_ref[:, MISC_IK:MISC_IK + IDX_DIM].T.astype(BF16)


def _kvprep(proj, kv_norm_g, KC):
    M = proj.shape[0]
    return pl.pallas_call(
        _kvprep_kernel,
        grid=(M // KC,),
        in_specs=[pl.BlockSpec((KC, DSA_LATENT), lambda i: (i, COL_DKV // DSA_LATENT)),
                  pl.BlockSpec((KC, 128), lambda i: (i, COL_MISC // 128)),
                  pl.BlockSpec((1, DSA_LATENT), lambda i: (0, 0))],
        out_specs=[pl.BlockSpec((KC, DSA_LATENT), lambda i: (i, 0)),
                   pl.BlockSpec((1, DSA_LATENT, KC), lambda i: (i, 0, 0)),
                   pl.BlockSpec((1, IDX_DIM, KC), lambda i: (i, 0, 0))],
        out_shape=[jax.ShapeDtypeStruct((M, DSA_LATENT), BF16),
                   jax.ShapeDtypeStruct((M // KC, DSA_LATENT, KC), BF16),
                   jax.ShapeDtypeStruct((M // KC, IDX_DIM, KC), BF16)],
        compiler_params=_params(("parallel",)),
        name="kvprep",
    )(proj, proj, kv_norm_g.reshape(1, DSA_LATENT))


def _dsa_kernel(dq_ref, iq_ref, misc_ref, ikt_ref, ckv_ref, ckvt_ref, wuk_ref, wuv_ref, o_ref,
                keys_ref, qlat_ref, m_ref, l_ref, acc_ref, thr_ref, tie_ref, *, S, KC, k_sel):
    QB = Q_BLOCK
    H = DSA_HEADS
    qb = pl.program_id(1)
    nkc = (qb * QB + QB + KC - 1) // KC
    t_pos = qb * QB + lax.broadcasted_iota(I32, (QB, 1), 0)
    lane_pos = lax.broadcasted_iota(I32, (1, KC), 1)

    iw = misc_ref[:, MISC_IW:MISC_IW + IDX_HEADS] * (IDX_HEADS ** -0.5) * (IDX_DIM ** -0.5)
    iq = iq_ref[...].astype(BF16)

    def score_body(kc, carry):
        k0 = pl.multiple_of(kc * KC, KC)
        ikc = ikt_ref[kc]
        score = jnp.zeros((QB, KC), F32)
        for h in range(IDX_HEADS):
            rel = _dot(iq[:, h * IDX_DIM:(h + 1) * IDX_DIM], ikc)
            rel = jnp.maximum(rel, 0.0)
            score = score + iw[:, h:h + 1] * rel
        bits = pltpu.bitcast(score, I32)
        keyv = bits ^ ((bits >> 31) & 0x7FFFFFFF)
        valid = (k0 + lane_pos) <= t_pos
        keys_ref[kc] = jnp.where(valid, keyv, INT_MIN)
        return carry

    lax.fori_loop(0, nkc, score_body, 0)

    def count(pred):
        def body(kc, acc):
            kv = keys_ref[kc]
            for g in range(KC // 128):
                acc = acc + jnp.where(pred(kv[:, g * 128:(g + 1) * 128], kc * KC + g * 128), 1.0, 0.0)
            return acc
        acc = lax.fori_loop(0, nkc, body, jnp.zeros((QB, 128), F32))
        return jnp.sum(acc, axis=1, keepdims=True)

    def bit_body(i, thr):
        cand = thr + (jnp.int32(1) << (31 - i))
        cnt = count(lambda kv, base: kv >= cand)
        return jnp.where(cnt >= k_sel, cand, thr)

    thr = lax.fori_loop(0, 32, bit_body, jnp.full((QB, 1), INT_MIN, I32))
    thr_ref[...] = thr
    n_gt = count(lambda kv, base: kv > thr)
    n_ge = count(lambda kv, base: kv >= thr)
    tie_ref[...] = jnp.full((QB, 1), S, I32)

    @pl.when(jnp.max(n_ge) > k_sel)
    def _():
        need = k_sel - n_gt
        lane = lax.broadcasted_iota(I32, (1, 128), 1)

        def pos_body(i, p):
            cand = p + (jnp.int32(1) << (S.bit_length() - 1 - i))
            cnt = count(lambda kv, base: (kv == thr) & ((base + lane) < cand))
            return jnp.where(cnt < need, cand, p)

        tie_ref[...] = lax.fori_loop(0, S.bit_length(), pos_body, jnp.zeros((QB, 1), I32))

    thr = thr_ref[...]
    tie = tie_ref[...]

    for h in range(H):
        ql = _dot(dq_ref[:, h * DSA_HEAD_DIM:(h + 1) * DSA_HEAD_DIM].astype(BF16), wuk_ref[h])
        qlat_ref[h * QB:(h + 1) * QB, :] = ql.astype(BF16)
    m_ref[...] = jnp.full(m_ref.shape, MASK_NEG, F32)
    l_ref[...] = jnp.zeros(l_ref.shape, F32)
    acc_ref[...] = jnp.zeros(acc_ref.shape, F32)

    log2e = 1.4426950408889634
    c_raw = (DSA_HEAD_DIM ** -0.5) * log2e
    HG = 2
    head = lax.broadcasted_iota(I32, (H * QB, 1), 0) // QB
    slope2 = jnp.exp2((head + 1).astype(F32) * (-8.0 / H)) * log2e

    def att_body(kc, carry):
        k0 = pl.multiple_of(kc * KC, KC)
        kv = ckv_ref[pl.ds(k0, KC), :]
        keyv = keys_ref[kc]
        kpos = k0 + lane_pos
        dist = t_pos - kpos
        sel = ((keyv > thr) | ((keyv == thr) & (kpos <= tie))) & (dist >= 0)
        dmask = jnp.where(sel, dist.astype(F32), MASK_DIST)
        dmask = jnp.concatenate([dmask] * HG, axis=0)
        kvt = ckvt_ref[kc]
        for g in range(H // HG):
            rows = slice(g * HG * QB, (g + 1) * HG * QB)
            lg = _dot(qlat_ref[rows], kvt) * c_raw - slope2[rows] * dmask
            m_old = m_ref[rows]
            m_new = jnp.maximum(m_old, jnp.max(lg, axis=1, keepdims=True))
            alpha = jnp.exp2(m_old - m_new)
            p = jnp.exp2(lg - m_new)
            l_ref[rows] = alpha * l_ref[rows] + jnp.sum(p, axis=1, keepdims=True)
            acc_ref[rows] = alpha * acc_ref[rows] + _dot(p.astype(BF16), kv)
            m_ref[rows] = m_new
        return carry

    lax.fori_loop(0, nkc, att_body, 0)

    for h in range(H):
        rows = slice(h * QB, (h + 1) * QB)
        o_lat = acc_ref[rows] / l_ref[rows]
        o_ref[:, h * DSA_HEAD_DIM:(h + 1) * DSA_HEAD_DIM] = _dot(o_lat.astype(BF16), wuv_ref[h]).astype(BF16)


def _dsa_key_chunk(S):
    return min(512, S)


def _dsa(proj, ikt, ckv, ckvt, w_uk, w_uv, B, S):
    M = proj.shape[0]
    QB = Q_BLOCK
    nqb = S // QB
    KC = _dsa_key_chunk(S)
    k_sel = min(DSA_TOPK, S // 4)
    HD = DSA_HEADS * DSA_HEAD_DIM
    return pl.pallas_call(
        functools.partial(_dsa_kernel, S=S, KC=KC, k_sel=k_sel),
        grid=(B, nqb),
        in_specs=[pl.BlockSpec((QB, HD), lambda b, q: (b * nqb + q, COL_DQ // HD)),
                  pl.BlockSpec((QB, IDX_HEADS * IDX_DIM), lambda b, q: (b * nqb + q, COL_IQ // (IDX_HEADS * IDX_DIM))),
                  pl.BlockSpec((QB, 128), lambda b, q: (b * nqb + q, COL_MISC // 128)),
                  pl.BlockSpec((S // KC, IDX_DIM, KC), lambda b, q: (b, 0, 0)),
                  pl.BlockSpec((S, DSA_LATENT), lambda b, q: (b, 0)),
                  pl.BlockSpec((S // KC, DSA_LATENT, KC), lambda b, q: (b, 0, 0)),
                  pl.BlockSpec((DSA_HEADS, DSA_HEAD_DIM, DSA_LATENT), lambda b, q: (0, 0, 0)),
                  pl.BlockSpec((DSA_HEADS, DSA_LATENT, DSA_HEAD_DIM), lambda b, q: (0, 0, 0))],
        out_specs=pl.BlockSpec((QB, HD), lambda b, q: (b * nqb + q, 0)),
        out_shape=jax.ShapeDtypeStruct((M, HD), BF16),
        scratch_shapes=[pltpu.VMEM((S // KC, QB, KC), I32),
                        pltpu.VMEM((DSA_HEADS * QB, DSA_LATENT), BF16),
                        pltpu.VMEM((DSA_HEADS * QB, 1), F32),
                        pltpu.VMEM((DSA_HEADS * QB, 1), F32),
                        pltpu.VMEM((DSA_HEADS * QB, DSA_LATENT), F32),
                        pltpu.VMEM((QB, 1), I32),
                        pltpu.VMEM((QB, 1), I32)],
        compiler_params=_params(("parallel", "arbitrary")),
        name="dsa",
    )(proj, proj, proj, ikt, ckv, ckvt, w_uk, w_uv)


def _merge_kernel(ya_ref, yb_ref, za_ref, zb_ref, x_ref, g1_ref, sc2_ref, sh2_ref, n2_ref,
                  wa_ref, wb_ref, wo_ref, wq_ref, k1_ref, k2_ref,
                  x1_ref, h2_ref, s1_ref, s2_ref):
    ya = _dot(ya_ref[...], wa_ref[...])
    yb = _dot(yb_ref[...], wb_ref[...])
    mix = jax.nn.sigmoid(za_ref[...]) * ya + jax.nn.sigmoid(zb_ref[...]) * yb
    y = _dot(mix.astype(BF16), wo_ref[...])
    x1 = x_ref[...] + g1_ref[0] * y
    x1_ref[...] = x1
    ms = jnp.mean(x1 * x1, axis=-1, keepdims=True)
    h2 = (x1 * lax.rsqrt(ms + NORM_EPS) * n2_ref[...]) * (1.0 + sc2_ref[0]) + sh2_ref[0]
    h2b = h2.astype(BF16)
    h2_ref[...] = h2b
    q = _dot(h2b, wq_ref[...]).astype(BF16)
    half = PEER_DKEY // 2
    for h in range(PEER_HEADS):
        s1_ref[h] = _dot_nt(k1_ref[...], q[:, h * PEER_DKEY:h * PEER_DKEY + half])
        s2_ref[h] = _dot_nt(k2_ref[...], q[:, h * PEER_DKEY + half:(h + 1) * PEER_DKEY])


def _merge(ya, yb, proj, x2, ada3, norm2_g, wa, wb, wo, wq, k1, k2, S):
    M, D = x2.shape
    tm = min(512, S)
    tpb = S // tm
    full = lambda shape: pl.BlockSpec(shape, lambda i: (0,) * len(shape))
    adaspec = lambda k: pl.BlockSpec((1, 1, D), lambda i: (i // tpb, 0, k))
    return pl.pallas_call(
        _merge_kernel,
        grid=(M // tm,),
        in_specs=[pl.BlockSpec((tm, D), lambda i: (i, 0)),
                  pl.BlockSpec((tm, D), lambda i: (i, 0)),
                  pl.BlockSpec((tm, D), lambda i: (i, COL_ZA // D)),
                  pl.BlockSpec((tm, D), lambda i: (i, COL_ZB // D)),
                  pl.BlockSpec((tm, D), lambda i: (i, 0)),
                  adaspec(2), adaspec(4), adaspec(3),
                  full((1, D)),
                  full((D, D)), full((D, D)), full((D, D)), full((D, PEER_HEADS * PEER_DKEY)),
                  full((PEER_NKEYS, PEER_DKEY // 2)), full((PEER_NKEYS, PEER_DKEY // 2))],
        out_specs=[pl.BlockSpec((tm, D), lambda i: (i, 0)),
                   pl.BlockSpec((tm, D), lambda i: (i, 0)),
                   pl.BlockSpec((PEER_HEADS, PEER_NKEYS, tm), lambda i: (0, 0, i)),
                   pl.BlockSpec((PEER_HEADS, PEER_NKEYS, tm), lambda i: (0, 0, i))],
        out_shape=[jax.ShapeDtypeStruct((M, D), F32),
                   jax.ShapeDtypeStruct((M, D), BF16),
                   jax.ShapeDtypeStruct((PEER_HEADS, PEER_NKEYS, M), F32),
                   jax.ShapeDtypeStruct((PEER_HEADS, PEER_NKEYS, M), F32)],
        compiler_params=_params(("parallel",)),
        name="merge",
    )(ya, yb, proj, proj, x2, ada3, ada3, ada3, norm2_g.reshape(1, D), wa, wb, wo, wq, k1, k2)


_PAIRS = [(j1, j2) for j1 in range(PEER_TOPK) for j2 in range(PEER_TOPK) if (j1 + 1) * (j2 + 1) <= PEER_TOPK]
_NCAND = -(-len(_PAIRS) // 8) * 8


def _extract_top(s, rounds, first_occurrence):
    n = s.shape[0]
    iota = lax.broadcasted_iota(I32, s.shape, 0)
    neg_iota = (-iota).astype(F32)
    vals, idxs = [], []
    for _ in range(rounds):
        m = jnp.max(s, axis=0, keepdims=True)
        eq = s == m
        if first_occurrence:
            idx = jnp.min(jnp.where(eq, iota, n), axis=0, keepdims=True)
            s = jnp.where(iota == idx, -jnp.inf, s)
            idx = idx.astype(F32)
        else:
            idx = -jnp.max(jnp.where(eq, neg_iota, -jnp.inf), axis=0, keepdims=True)
            s = jnp.where(eq, -jnp.inf, s)
        vals.append(m)
        idxs.append(idx)
    removed = jnp.sum(jnp.where(s == -jnp.inf, 1.0, 0.0), axis=0, keepdims=True)
    return vals, idxs, removed


def _route_block(s1_ref, s2_ref, ia_ref, ib_ref, g_ref, ta_ref, tb_ref, tg_ref, first_occurrence):
    R = s1_ref.shape[2]
    K = PEER_TOPK
    pad = _NCAND - len(_PAIRS)

    def head_body(h, bad):
        v1, i1, r1 = _extract_top(s1_ref[h], K, first_occurrence)
        v2, i2, r2 = _extract_top(s2_ref[h], K, first_occurrence)
        cand = jnp.concatenate([v1[a] + v2[b] for a, b in _PAIRS] + [jnp.full((pad, R), -jnp.inf, F32)], axis=0)
        ca = jnp.concatenate([i1[a] for a, b in _PAIRS] + [jnp.zeros((pad, R), F32)], axis=0)
        cb = jnp.concatenate([i2[b] for a, b in _PAIRS] + [jnp.zeros((pad, R), F32)], axis=0)
        best, pos, rc = _extract_top(cand, K, first_occurrence)
        iota = lax.broadcasted_iota(I32, cand.shape, 0).astype(F32)
        e = [jnp.exp(bv - best[0]) for bv in best]
        denom = e[0]
        for j in range(1, K):
            denom = denom + e[j]
        ia_rows, ib_rows, g_rows = [], [], []
        for j in range(K):
            hit = iota == pos[j]
            ia_rows.append(jnp.sum(jnp.where(hit, ca, 0.0), axis=0, keepdims=True))
            ib_rows.append(jnp.sum(jnp.where(hit, cb, 0.0), axis=0, keepdims=True))
            g_rows.append(e[j] / denom)
        rows = pl.ds(pl.multiple_of(h * K, K), K)
        ta_ref[rows, :] = jnp.concatenate(ia_rows, axis=0)
        tb_ref[rows, :] = jnp.concatenate(ib_rows, axis=0)
        tg_ref[rows, :] = jnp.concatenate(g_rows, axis=0)
        return bad + jnp.where((r1 != K) | (r2 != K) | (rc != K + pad), 1.0, 0.0)

    bad = lax.fori_loop(0, PEER_HEADS, head_body, jnp.zeros((1, R), F32))
    ia_ref[...] = ta_ref[...].T
    ib_ref[...] = tb_ref[...].T
    g_ref[...] = tg_ref[...].T
    return bad


def _route_kernel(s1_ref, s2_ref, ia_ref, ib_ref, g_ref, ta_ref, tb_ref, tg_ref):
    scratch = (ta_ref, tb_ref, tg_ref)
    bad = _route_block(s1_ref, s2_ref, ia_ref, ib_ref, g_ref, *scratch, first_occurrence=False)

    @pl.when(jnp.max(bad) > 0.0)
    def _():
        _route_block(s1_ref, s2_ref, ia_ref, ib_ref, g_ref, *scratch, first_occurrence=True)


def _route(s1, s2):
    M = s1.shape[2]
    R = 512
    NS = PEER_HEADS * PEER_TOPK
    spec_in = pl.BlockSpec((PEER_HEADS, PEER_NKEYS, R), lambda i: (0, 0, i))
    spec_out = pl.BlockSpec((R, NS), lambda i: (i, 0))
    return pl.pallas_call(
        _route_kernel,
        grid=(M // R,),
        in_specs=[spec_in, spec_in],
        out_specs=[spec_out, spec_out, spec_out],
        out_shape=[jax.ShapeDtypeStruct((M, NS), F32)] * 3,
        scratch_shapes=[pltpu.VMEM((NS, R), F32)] * 3,
        compiler_params=_params(("parallel",)),
        name="route",
    )(s1, s2)


GM_PITCH = PEER_NKEYS // 2 + 4

def _peer_kernel(h2_ref, ia_ref, ib_ref, g_ref, u_ref, v_ref, x1_ref, g2_ref, fg_ref, o_ref,
                 gm_ref, acc_ref, *, T, AE, final_norm):
    e = pl.program_id(1)
    NK = PEER_NKEYS
    NA = NK // 2
    steps_per_build = NA // AE
    TOK_UNROLL = 32

    @pl.when(e == 0)
    def _():
        acc_ref[...] = jnp.zeros_like(acc_ref)

    @pl.when(e % steps_per_build == 0)
    def _():
        a_base = (e // steps_per_build) * NA
        sub_a = (lax.broadcasted_iota(I32, (NA, NK), 0) + a_base).astype(F32)
        sub_b = lax.broadcasted_iota(I32, (NK, NK), 0).astype(F32)

        zero = jnp.zeros((NA, NK), BF16)

        def tok_body(t8, carry):
            r0 = pl.multiple_of(t8 * TOK_UNROLL, TOK_UNROLL)
            base = pl.multiple_of(r0 * GM_PITCH, TOK_UNROLL)
            ia8 = ia_ref[pl.ds(r0, TOK_UNROLL), :]
            ib8 = ib_ref[pl.ds(r0, TOK_UNROLL), :]
            g8 = g_ref[pl.ds(r0, TOK_UNROLL), :]
            for j in range(0, TOK_UNROLL, 2):
                am = [jnp.where(sub_a == ia8[k:k + 1, :], g8[k:k + 1, :], 0.0).astype(BF16) for k in (j, j + 1)]
                bm = [jnp.where(sub_b == ib8[k:k + 1, :], 1.0, 0.0).astype(BF16) for k in (j, j + 1)]
                lhs = jnp.concatenate([jnp.concatenate([am[0], zero], axis=1),
                                       jnp.concatenate([zero, am[1]], axis=1)], axis=0)
                pair = _dot_nt(lhs, jnp.concatenate(bm, axis=1))
                gm_ref[pl.ds(base + j * GM_PITCH, NA), :] = pair[:NA]
                gm_ref[pl.ds(base + (j + 1) * GM_PITCH, NA), :] = pair[NA:]
            return carry

        lax.fori_loop(0, T // TOK_UNROLL, tok_body, 0)

    a_loc = (e % steps_per_build) * AE
    s = _dot_nt(h2_ref[...], u_ref[...])
    gt = jnp.concatenate([gm_ref[pl.ds(a_loc + j, T, stride=GM_PITCH), :] for j in range(AE)], axis=1)
    act = 0.5 * s * (1.0 + lax.erf(s * (0.5 ** 0.5)))
    acc_ref[...] += _dot((act * gt).astype(BF16), v_ref[...])

    @pl.when(e == pl.num_programs(1) - 1)
    def _():
        x2 = x1_ref[...] + g2_ref[0] * acc_ref[...]
        if final_norm:
            ms = jnp.mean(x2 * x2, axis=-1, keepdims=True)
            x2 = x2 * lax.rsqrt(ms + NORM_EPS) * fg_ref[...]
        o_ref[...] = x2


def _peer(h2, ia, ib, g, u, v, x1, ada3, final_g, S, final_norm):
    M, D = x1.shape
    T = min(512, S)
    AE = 8
    ET = AE * PEER_NKEYS
    NE = u.shape[0]
    NS = PEER_HEADS * PEER_TOPK
    tpb = S // T
    return pl.pallas_call(
        functools.partial(_peer_kernel, T=T, AE=AE, final_norm=final_norm),
        grid=(M // T, NE // ET),
        in_specs=[pl.BlockSpec((T, D), lambda i, e: (i, 0)),
                  pl.BlockSpec((T, NS), lambda i, e: (i, 0)),
                  pl.BlockSpec((T, NS), lambda i, e: (i, 0)),
                  pl.BlockSpec((T, NS), lambda i, e: (i, 0)),
                  pl.BlockSpec((ET, D), lambda i, e: (e, 0)),
                  pl.BlockSpec((ET, D), lambda i, e: (e, 0)),
                  pl.BlockSpec((T, D), lambda i, e: (i, 0)),
                  pl.BlockSpec((1, 1, D), lambda i, e: (i // tpb, 0, 5)),
                  pl.BlockSpec((1, D), lambda i, e: (0, 0))],
        out_specs=pl.BlockSpec((T, D), lambda i, e: (i, 0)),
        out_shape=jax.ShapeDtypeStruct((M, D), F32),
        scratch_shapes=[pltpu.VMEM((T * GM_PITCH, PEER_NKEYS), F32),
                        pltpu.VMEM((T, D), F32)],
        compiler_params=_params(("parallel", "arbitrary")),
        name="peer",
    )(h2, ia, ib, g, u, v, x1, ada3, final_g.reshape(1, D))


def _cat_w_in(w_in):
    sizes = (GLA_DK, GLA_DK, GLA_DV, GLA_DV, GLA_GATE_RANK, DSA_HEADS * DSA_HEAD_DIM, DSA_LATENT,
             IDX_HEADS * IDX_DIM, IDX_DIM, IDX_HEADS, D_MODEL, D_MODEL)
    offs = [0]
    for s in sizes:
        offs.append(offs[-1] + s)
    gq, gk, gv, gr, glow, dq, dkv, iq, ik, iw, za, zb = [w_in[:, offs[i]:offs[i + 1]] for i in range(len(sizes))]
    pad_misc = jnp.zeros((w_in.shape[0], 128 - (GLA_GATE_RANK + IDX_DIM + IDX_HEADS)), w_in.dtype)
    pad_end = jnp.zeros((w_in.shape[0], PROJ_COLS - (COL_MISC + 128)), w_in.dtype)
    return jnp.concatenate([gv, gr, dq, za, zb, gq, gk, iq, dkv, glow, ik, iw, pad_misc, pad_end], axis=1).astype(BF16)


def kernel(x, c, w_ada, b_ada, norm1_g, w_in, gla_w_gate_up, gla_b_gate, gla_norm_g, dsa_kv_norm_g, dsa_w_uk,
           dsa_w_uv, w_branch_a, w_branch_b, w_out, norm2_g, peer_w_q, peer_sub_keys_1, peer_sub_keys_2,
           peer_u, peer_v, final_norm_g):
    B, S, D = x.shape
    depth = w_ada.shape[0]
    x2 = x.reshape(B * S, D)
    for l in range(depth):
        ada3 = _ada(c, w_ada[l], b_ada[l]).reshape(B, 1, 6 * D)
        proj = _proj(x2, ada3, norm1_g[l], _cat_w_in(w_in[l]), S)
        ya = _gla(proj, gla_w_gate_up[l], gla_b_gate[l], gla_norm_g[l], B, S)
        ckv, ckvt, ikt = _kvprep(proj, dsa_kv_norm_g[l], _dsa_key_chunk(S))
        yb = _dsa(proj, ikt, ckv, ckvt, dsa_w_uk[l].astype(BF16), dsa_w_uv[l].astype(BF16), B, S)
        x1, h2, s1, s2 = _merge(ya, yb, proj, x2, ada3, norm2_g[l],
                                w_branch_a[l].astype(BF16), w_branch_b[l].astype(BF16), w_out[l].astype(BF16),
                                peer_w_q[l].astype(BF16), peer_sub_keys_1[l].astype(BF16),
                                peer_sub_keys_2[l].astype(BF16), S)
        ia, ib, g = _route(s1, s2)
        x2 = _peer(h2, ia, ib, g, peer_u[l].astype(BF16), peer_v[l].astype(BF16), x1, ada3, final_norm_g, S,
                   final_norm=(l == depth - 1))
    return x2.reshape(B, S, D)
```

```python
import functools

import jax
import jax.numpy as jnp
from jax import lax
from jax.experimental import pallas as pl
from jax.experimental.pallas import tpu as pltpu

F32 = jnp.float32
BF16 = jnp.bfloat16
I32 = jnp.int32

D_MODEL = 1024
GLA_HEADS = 4
GLA_DK = D_MODEL // 2
GLA_DV = D_MODEL
GLA_GATE_RANK = 16
GLA_TAU = 16.0
GLA_CHUNK = 64
DSA_HEADS = 8
DSA_HEAD_DIM = 128
DSA_LATENT = 256
IDX_HEADS = 8
IDX_DIM = 64
DSA_TOPK = 256
Q_BLOCK = 128
PEER_HEADS = 8
PEER_NKEYS = 128
PEER_DKEY = 256
PEER_TOPK = 16
NORM_EPS = 1e-6

INT_MIN = -(2 ** 31)
MASK_NEG = -1e30
MASK_DIST = 1e30

COL_GV, COL_GR, COL_DQ, COL_ZA, COL_ZB = 0, 1024, 2048, 3072, 4096
COL_GQ, COL_GK, COL_IQ, COL_DKV, COL_MISC = 5120, 5632, 6144, 6656, 6912
PROJ_COLS = 7168
MISC_GLOW, MISC_IK, MISC_IW = 0, 16, 80

VMEM_LIMIT = 56 * 1024 * 1024


def _params(sem):
    return pltpu.CompilerParams(dimension_semantics=sem, vmem_limit_bytes=VMEM_LIMIT)


def _dot(a, b):
    return jnp.dot(a, b, preferred_element_type=F32)


def _dot_nt(a, b):
    return lax.dot_general(a, b, (((1,), (1,)), ((), ())), preferred_element_type=F32)


def _dot_tn(a, b):
    return lax.dot_general(a, b, (((0,), (0,)), ((), ())), preferred_element_type=F32)


def _ada_kernel(c_ref, w_ref, b_ref, o_ref):
    c = c_ref[...]
    a = c * jax.nn.sigmoid(c)
    o_ref[...] = _dot(a.astype(BF16), w_ref[...].astype(BF16)) + b_ref[...]


def _ada(c, w_ada, b_ada):
    B, D = c.shape
    N = w_ada.shape[1]
    tn = 1024
    return pl.pallas_call(
        _ada_kernel,
        grid=(N // tn,),
        in_specs=[pl.BlockSpec((B, D), lambda j: (0, 0)),
                  pl.BlockSpec((D, tn), lambda j: (0, j)),
                  pl.BlockSpec((1, tn), lambda j: (0, j))],
        out_specs=pl.BlockSpec((B, tn), lambda j: (0, j)),
        out_shape=jax.ShapeDtypeStruct((B, N), F32),
        compiler_params=_params(("parallel",)),
        name="ada",
    )(c, w_ada, b_ada.reshape(1, N))


def _proj_kernel(x_ref, sc_ref, sh_ref, g_ref, w_ref, o_ref, h_ref):
    @pl.when(pl.program_id(1) == 0)
    def _():
        x = x_ref[...]
        ms = jnp.mean(x * x, axis=-1, keepdims=True)
        y = x * lax.rsqrt(ms + NORM_EPS) * g_ref[...]
        h_ref[...] = (y * (1.0 + sc_ref[0]) + sh_ref[0]).astype(BF16)

    o_ref[...] = _dot(h_ref[...], w_ref[...]).astype(BF16)


def _proj(x2, ada3, norm_g, w_cat, S):
    M, D = x2.shape
    N = w_cat.shape[1]
    tm, tn = min(2048, S), 512
    tpb = S // tm
    return pl.pallas_call(
        _proj_kernel,
        grid=(M // tm, N // tn),
        in_specs=[pl.BlockSpec((tm, D), lambda i, j: (i, 0)),
                  pl.BlockSpec((1, 1, D), lambda i, j: (i // tpb, 0, 1)),
                  pl.BlockSpec((1, 1, D), lambda i, j: (i // tpb, 0, 0)),
                  pl.BlockSpec((1, D), lambda i, j: (0, 0)),
                  pl.BlockSpec((D, tn), lambda i, j: (0, j))],
        out_specs=pl.BlockSpec((tm, tn), lambda i, j: (i, j)),
        out_shape=jax.ShapeDtypeStruct((M, N), BF16),
        scratch_shapes=[pltpu.VMEM((tm, D), BF16)],
        compiler_params=_params(("parallel", "arbitrary")),
        name="proj",
    )(x2, ada3, ada3, norm_g.reshape(1, D), w_cat)


def _gla_kernel(q_ref, k_ref, v_ref, r_ref, misc_ref, wg_ref, bg_ref, ng_ref, o_ref, st_ref, *, n_chunks):
    C = GLA_CHUNK
    H = GLA_HEADS
    dkh = GLA_DK // H
    dvh = GLA_DV // H

    @pl.when(pl.program_id(1) == 0)
    def _():
        st_ref[...] = jnp.zeros_like(st_ref)

    wg = wg_ref[...].astype(BF16)
    bg = bg_ref[...]
    ng = ng_ref[...]
    row = lax.broadcasted_iota(I32, (C, C), 0)
    col = lax.broadcasted_iota(I32, (C, C), 1)
    causal = row >= col
    tril = jnp.where(causal, 1.0, 0.0).astype(BF16)

    st = [st_ref[h] for h in range(H)]
    for c in range(n_chunks):
        sl = slice(c * C, (c + 1) * C)
        glow = misc_ref[sl, MISC_GLOW:MISC_GLOW + GLA_GATE_RANK]
        gate = _dot(glow, wg) + bg
        la = (jnp.minimum(gate, 0.0) - jnp.log1p(jnp.exp(-jnp.abs(gate)))) * (1.0 / GLA_TAU)
        la_hi = la.astype(BF16)
        la_lo = (la - la_hi.astype(F32)).astype(BF16)
        b = _dot(tril, la_hi) + _dot(tril, la_lo)
        b_last = b[C - 1:C, :]
        q_dec = (q_ref[sl, :].astype(F32) * (dkh ** -0.5) * jnp.exp(b)).astype(BF16)
        k = k_ref[sl, :].astype(F32)
        k_inv = (k * jnp.exp(-b)).astype(BF16)
        k_end = (k * jnp.exp(b_last - b)).astype(BF16)
        decay = jnp.exp(b_last)
        v = v_ref[sl, :]
        r = r_ref[sl, :].astype(F32)
        gate_r = r * jax.nn.sigmoid(r)
        for h in range(H):
            ks = slice(h * dkh, (h + 1) * dkh)
            vs = slice(h * dvh, (h + 1) * dvh)
            attn = jnp.where(causal, _dot_nt(q_dec[:, ks], k_inv[:, ks]), 0.0)
            o = _dot(attn.astype(BF16), v[:, vs]) + _dot_nt(q_dec[:, ks], st[h].astype(BF16))
            st[h] = decay[:, ks] * st[h] + _dot_tn(v[:, vs], k_end[:, ks])
            ms = jnp.mean(o * o, axis=-1, keepdims=True)
            on = o * lax.rsqrt(ms + NORM_EPS) * ng[:, vs]
            o_ref[sl, vs] = (on * gate_r[:, vs]).astype(BF16)
    for h in range(H):
        st_ref[h] = st[h]


def _gla(proj, w_gate_up, b_gate, norm_g, B, S):
    M = proj.shape[0]
    rb = min(256, S)
    nrb = S // rb
    dkh = GLA_DK // GLA_HEADS
    dvh = GLA_DV // GLA_HEADS
    rowmap = lambda b, c: b * nrb + c
    return pl.pallas_call(
        functools.partial(_gla_kernel, n_chunks=rb // GLA_CHUNK),
        grid=(B, nrb),
        in_specs=[pl.BlockSpec((rb, GLA_DK), lambda b, c: (rowmap(b, c), COL_GQ // GLA_DK)),
                  pl.BlockSpec((rb, GLA_DK), lambda b, c: (rowmap(b, c), COL_GK // GLA_DK)),
                  pl.BlockSpec((rb, GLA_DV), lambda b, c: (rowmap(b, c), COL_GV // GLA_DV)),
                  pl.BlockSpec((rb, GLA_DV), lambda b, c: (rowmap(b, c), COL_GR // GLA_DV)),
                  pl.BlockSpec((rb, 128), lambda b, c: (rowmap(b, c), COL_MISC // 128)),
                  pl.BlockSpec((GLA_GATE_RANK, GLA_DK), lambda b, c: (0, 0)),
                  pl.BlockSpec((1, GLA_DK), lambda b, c: (0, 0)),
                  pl.BlockSpec((1, GLA_DV), lambda b, c: (0, 0))],
        out_specs=pl.BlockSpec((rb, GLA_DV), lambda b, c: (rowmap(b, c), 0)),
        out_shape=jax.ShapeDtypeStruct((M, GLA_DV), BF16),
        scratch_shapes=[pltpu.VMEM((GLA_HEADS, dvh, dkh), F32)],
        compiler_params=_params(("parallel", "arbitrary")),
        name="gla",
    )(proj, proj, proj, proj, proj, w_gate_up, b_gate.reshape(1, GLA_DK), norm_g.reshape(1, GLA_DV))


def _kvprep_kernel(kv_ref, misc_ref, g_ref, ckv_ref, ckvt_ref, ikt_ref):
    x = kv_ref[...].astype(F32)
    ms = jnp.mean(x * x, axis=-1, keepdims=True)
    c = x * lax.rsqrt(ms + NORM_EPS) * g_ref[...]
    ckv_ref[...] = c.astype(BF16)
    ckvt_ref[0] = c.T.astype(BF16)
    ikt_ref[0] = misc_ref[:, MISC_IK:MISC_IK + IDX_DIM].astype(F32).T.astype(BF16)


def _kvprep(proj, kv_norm_g, KC):
    M = proj.shape[0]
    return pl.pallas_call(
        _kvprep_kernel,
        grid=(M // KC,),
        in_specs=[pl.BlockSpec((KC, DSA_LATENT), lambda i: (i, COL_DKV // DSA_LATENT)),
                  pl.BlockSpec((KC, 128), lambda i: (i, COL_MISC // 128)),
                  pl.BlockSpec((1, DSA_LATENT), lambda i: (0, 0))],
        out_specs=[pl.BlockSpec((KC, DSA_LATENT), lambda i: (i, 0)),
                   pl.BlockSpec((1, DSA_LATENT, KC), lambda i: (i, 0, 0)),
                   pl.BlockSpec((1, IDX_DIM, KC), lambda i: (i, 0, 0))],
        out_shape=[jax.ShapeDtypeStruct((M, DSA_LATENT), BF16),
                   jax.ShapeDtypeStruct((M // KC, DSA_LATENT, KC), BF16),
                   jax.ShapeDtypeStruct((M // KC, IDX_DIM, KC), BF16)],
        compiler_params=_params(("parallel",)),
        name="kvprep",
    )(proj, proj, kv_norm_g.reshape(1, DSA_LATENT))


def _dsa_kernel(dq_ref, iq_ref, misc_ref, ikt_ref, ckv_ref, ckvt_ref, wuk_ref, wuv_ref, o_ref,
                keys_ref, qlat_ref, m_ref, l_ref, acc_ref, thr_ref, tie_ref, *, S, KC, k_sel):
    QB = Q_BLOCK
    H = DSA_HEADS
    qb = pl.program_id(1)
    nkc = (qb * QB + QB + KC - 1) // KC
    t_pos = qb * QB + lax.broadcasted_iota(I32, (QB, 1), 0)
    lane_pos = lax.broadcasted_iota(I32, (1, KC), 1)

    iw = misc_ref[:, MISC_IW:MISC_IW + IDX_HEADS].astype(F32) * (IDX_HEADS ** -0.5) * (IDX_DIM ** -0.5)
    iq = iq_ref[...]

    def score_body(kc, carry):
        k0 = pl.multiple_of(kc * KC, KC)
        ikc = ikt_ref[kc]
        score = jnp.zeros((QB, KC), F32)
        for h in range(IDX_HEADS):
            rel = _dot(iq[:, h * IDX_DIM:(h + 1) * IDX_DIM], ikc)
            rel = jnp.maximum(rel, 0.0)
            score = score + iw[:, h:h + 1] * rel
        bits = pltpu.bitcast(score, I32)
        keyv = bits ^ ((bits >> 31) & 0x7FFFFFFF)
        valid = (k0 + lane_pos) <= t_pos
        keys_ref[kc] = jnp.where(valid, keyv, INT_MIN)
        return carry

    lax.fori_loop(0, nkc, score_body, 0)

    def count(pred):
        def body(kc, acc):
            kv = keys_ref[kc]
            for g in range(KC // 128):
                acc = acc + jnp.where(pred(kv[:, g * 128:(g + 1) * 128], kc * KC + g * 128), 1.0, 0.0)
            return acc
        acc = lax.fori_loop(0, nkc, body, jnp.zeros((QB, 128), F32))
        return jnp.sum(acc, axis=1, keepdims=True)

    def bit_body(i, thr):
        cand = thr + (jnp.int32(1) << (31 - i))
        cnt = count(lambda kv, base: kv >= cand)
        return jnp.where(cnt >= k_sel, cand, thr)

    thr = lax.fori_loop(0, 32, bit_body, jnp.full((QB, 1), INT_MIN, I32))
    thr_ref[...] = thr
    n_gt = count(lambda kv, base: kv > thr)
    n_ge = count(lambda kv, base: kv >= thr)
    tie_ref[...] = jnp.full((QB, 1), S, I32)

    @pl.when(jnp.max(n_ge) > k_sel)
    def _():
        need = k_sel - n_gt
        lane = lax.broadcasted_iota(I32, (1, 128), 1)

        def pos_body(i, p):
            cand = p + (jnp.int32(1) << (S.bit_length() - 1 - i))
            cnt = count(lambda kv, base: (kv == thr) & ((base + lane) < cand))
            return jnp.where(cnt < need, cand, p)

        tie_ref[...] = lax.fori_loop(0, S.bit_length(), pos_body, jnp.zeros((QB, 1), I32))

    thr = thr_ref[...]
    tie = tie_ref[...]

    for h in range(H):
        ql = _dot(dq_ref[:, h * DSA_HEAD_DIM:(h + 1) * DSA_HEAD_DIM], wuk_ref[h])
        qlat_ref[h * QB:(h + 1) * QB, :] = ql.astype(BF16)
    m_ref[...] = jnp.full(m_ref.shape, MASK_NEG, F32)
    l_ref[...] = jnp.zeros(l_ref.shape, F32)
    acc_ref[...] = jnp.zeros(acc_ref.shape, F32)

    log2e = 1.4426950408889634
    c_raw = (DSA_HEAD_DIM ** -0.5) * log2e
    HG = 2
    head = lax.broadcasted_iota(I32, (H * QB, 1), 0) // QB
    slope2 = jnp.exp2((head + 1).astype(F32) * (-8.0 / H)) * log2e

    def att_body(kc, carry):
        k0 = pl.multiple_of(kc * KC, KC)
        kv = ckv_ref[pl.ds(k0, KC), :]
        keyv = keys_ref[kc]
        kpos = k0 + lane_pos
        dist = t_pos - kpos
        sel = ((keyv > thr) | ((keyv == thr) & (kpos <= tie))) & (dist >= 0)
        dmask = jnp.where(sel, dist.astype(F32), MASK_DIST)
        dmask = jnp.concatenate([dmask] * HG, axis=0)
        kvt = ckvt_ref[kc]
        for g in range(H // HG):
            rows = slice(g * HG * QB, (g + 1) * HG * QB)
            lg = _dot(qlat_ref[rows], kvt) * c_raw - slope2[rows] * dmask
            m_old = m_ref[rows]
            m_new = jnp.maximum(m_old, jnp.max(lg, axis=1, keepdims=True))
            alpha = jnp.exp2(m_old - m_new)
            p = jnp.exp2(lg - m_new)
            l_ref[rows] = alpha * l_ref[rows] + jnp.sum(p, axis=1, keepdims=True)
            acc_ref[rows] = alpha * acc_ref[rows] + _dot(p.astype(BF16), kv)
            m_ref[rows] = m_new
        return carry

    lax.fori_loop(0, nkc, att_body, 0)

    for h in range(H):
        rows = slice(h * QB, (h + 1) * QB)
        o_lat = acc_ref[rows] / l_ref[rows]
        o_ref[:, h * DSA_HEAD_DIM:(h + 1) * DSA_HEAD_DIM] = _dot(o_lat.astype(BF16), wuv_ref[h]).astype(BF16)


def _dsa_key_chunk(S):
    return min(512, S)


def _dsa(proj, ikt, ckv, ckvt, w_uk, w_uv, B, S):
    M = proj.shape[0]
    QB = Q_BLOCK
    nqb = S // QB
    KC = _dsa_key_chunk(S)
    k_sel = min(DSA_TOPK, S // 4)
    HD = DSA_HEADS * DSA_HEAD_DIM
    return pl.pallas_call(
        functools.partial(_dsa_kernel, S=S, KC=KC, k_sel=k_sel),
        grid=(B, nqb),
        in_specs=[pl.BlockSpec((QB, HD), lambda b, q: (b * nqb + q, COL_DQ // HD)),
                  pl.BlockSpec((QB, IDX_HEADS * IDX_DIM), lambda b, q: (b * nqb + q, COL_IQ // (IDX_HEADS * IDX_DIM))),
                  pl.BlockSpec((QB, 128), lambda b, q: (b * nqb + q, COL_MISC // 128)),
                  pl.BlockSpec((S // KC, IDX_DIM, KC), lambda b, q: (b, 0, 0)),
                  pl.BlockSpec((S, DSA_LATENT), lambda b, q: (b, 0)),
                  pl.BlockSpec((S // KC, DSA_LATENT, KC), lambda b, q: (b, 0, 0)),
                  pl.BlockSpec((DSA_HEADS, DSA_HEAD_DIM, DSA_LATENT), lambda b, q: (0, 0, 0)),
                  pl.BlockSpec((DSA_HEADS, DSA_LATENT, DSA_HEAD_DIM), lambda b, q: (0, 0, 0))],
        out_specs=pl.BlockSpec((QB, HD), lambda b, q: (b * nqb + q, 0)),
        out_shape=jax.ShapeDtypeStruct((M, HD), BF16),
        scratch_shapes=[pltpu.VMEM((S // KC, QB, KC), I32),
                        pltpu.VMEM((DSA_HEADS * QB, DSA_LATENT), BF16),
                        pltpu.VMEM((DSA_HEADS * QB, 1), F32),
                        pltpu.VMEM((DSA_HEADS * QB, 1), F32),
                        pltpu.VMEM((DSA_HEADS * QB, DSA_LATENT), F32),
                        pltpu.VMEM((QB, 1), I32),
                        pltpu.VMEM((QB, 1), I32)],
        compiler_params=_params(("parallel", "arbitrary")),
        name="dsa",
    )(proj, proj, proj, ikt, ckv, ckvt, w_uk, w_uv)


def _merge_kernel(ya_ref, yb_ref, za_ref, zb_ref, x_ref, g1_ref, sc2_ref, sh2_ref, n2_ref,
                  wa_ref, wb_ref, wo_ref, wq_ref, k1_ref, k2_ref,
                  x1_ref, h2_ref, s1_ref, s2_ref):
    ya = _dot(ya_ref[...], wa_ref[...])
    yb = _dot(yb_ref[...], wb_ref[...])
    mix = jax.nn.sigmoid(za_ref[...].astype(F32)) * ya + jax.nn.sigmoid(zb_ref[...].astype(F32)) * yb
    y = _dot(mix.astype(BF16), wo_ref[...])
    x1 = x_ref[...] + g1_ref[0] * y
    x1_ref[...] = x1
    ms = jnp.mean(x1 * x1, axis=-1, keepdims=True)
    h2 = (x1 * lax.rsqrt(ms + NORM_EPS) * n2_ref[...]) * (1.0 + sc2_ref[0]) + sh2_ref[0]
    h2b = h2.astype(BF16)
    h2_ref[...] = h2b
    q = _dot(h2b, wq_ref[...]).astype(BF16)
    half = PEER_DKEY // 2
    for h in range(PEER_HEADS):
        s1_ref[h] = _dot_nt(k1_ref[...], q[:, h * PEER_DKEY:h * PEER_DKEY + half])
        s2_ref[h] = _dot_nt(k2_ref[...], q[:, h * PEER_DKEY + half:(h + 1) * PEER_DKEY])


def _merge(ya, yb, proj, x2, ada3, norm2_g, wa, wb, wo, wq, k1, k2, S):
    M, D = x2.shape
    tm = min(512, S)
    tpb = S // tm
    full = lambda shape: pl.BlockSpec(shape, lambda i: (0,) * len(shape))
    adaspec = lambda k: pl.BlockSpec((1, 1, D), lambda i: (i // tpb, 0, k))
    return pl.pallas_call(
        _merge_kernel,
        grid=(M // tm,),
        in_specs=[pl.BlockSpec((tm, D), lambda i: (i, 0)),
                  pl.BlockSpec((tm, D), lambda i: (i, 0)),
                  pl.BlockSpec((tm, D), lambda i: (i, COL_ZA // D)),
                  pl.BlockSpec((tm, D), lambda i: (i, COL_ZB // D)),
                  pl.BlockSpec((tm, D), lambda i: (i, 0)),
                  adaspec(2), adaspec(4), adaspec(3),
                  full((1, D)),
                  full((D, D)), full((D, D)), full((D, D)), full((D, PEER_HEADS * PEER_DKEY)),
                  full((PEER_NKEYS, PEER_DKEY // 2)), full((PEER_NKEYS, PEER_DKEY // 2))],
        out_specs=[pl.BlockSpec((tm, D), lambda i: (i, 0)),
                   pl.BlockSpec((tm, D), lambda i: (i, 0)),
                   pl.BlockSpec((PEER_HEADS, PEER_NKEYS, tm), lambda i: (0, 0, i)),
                   pl.BlockSpec((PEER_HEADS, PEER_NKEYS, tm), lambda i: (0, 0, i))],
        out_shape=[jax.ShapeDtypeStruct((M, D), F32),
                   jax.ShapeDtypeStruct((M, D), BF16),
                   jax.ShapeDtypeStruct((PEER_HEADS, PEER_NKEYS, M), F32),
                   jax.ShapeDtypeStruct((PEER_HEADS, PEER_NKEYS, M), F32)],
        compiler_params=_params(("parallel",)),
        name="merge",
    )(ya, yb, proj, proj, x2, ada3, ada3, ada3, norm2_g.reshape(1, D), wa, wb, wo, wq, k1, k2)


_PAIRS = [(j1, j2) for j1 in range(PEER_TOPK) for j2 in range(PEER_TOPK) if (j1 + 1) * (j2 + 1) <= PEER_TOPK]
_NCAND = -(-len(_PAIRS) // 8) * 8


def _extract_top(s, rounds):
    neg_iota = (-lax.broadcasted_iota(I32, s.shape, 0)).astype(F32)
    vals, idxs = [], []
    for _ in range(rounds):
        m = jnp.max(s, axis=0, keepdims=True)
        neg_idx = jnp.max(jnp.where(s == m, neg_iota, -jnp.inf), axis=0, keepdims=True)
        s = jnp.where(neg_iota == neg_idx, -jnp.inf, s)
        vals.append(m)
        idxs.append(-neg_idx)
    return vals, idxs


def _route_kernel(s1_ref, s2_ref, ia_ref, ib_ref, g_ref, ta_ref, tb_ref, tg_ref):
    R = s1_ref.shape[2]
    K = PEER_TOPK
    pad = _NCAND - len(_PAIRS)

    def head_body(h, carry):
        v1, i1 = _extract_top(s1_ref[h], K)
        v2, i2 = _extract_top(s2_ref[h], K)
        cand = jnp.concatenate([v1[a] + v2[b] for a, b in _PAIRS] + [jnp.full((pad, R), -jnp.inf, F32)], axis=0)
        ca = jnp.concatenate([i1[a] for a, b in _PAIRS] + [jnp.zeros((pad, R), F32)], axis=0)
        cb = jnp.concatenate([i2[b] for a, b in _PAIRS] + [jnp.zeros((pad, R), F32)], axis=0)
        best, pos = _extract_top(cand, K)
        iota = lax.broadcasted_iota(I32, cand.shape, 0).astype(F32)
        e = [jnp.exp(bv - best[0]) for bv in best]
        denom = e[0]
        for j in range(1, K):
            denom = denom + e[j]
        ia_rows, ib_rows, g_rows = [], [], []
        for j in range(K):
            hit = iota == pos[j]
            ia_rows.append(jnp.sum(jnp.where(hit, ca, 0.0), axis=0, keepdims=True))
            ib_rows.append(jnp.sum(jnp.where(hit, cb, 0.0), axis=0, keepdims=True))
            g_rows.append(e[j] / denom)
        rows = pl.ds(pl.multiple_of(h * K, K), K)
        ta_ref[rows, :] = jnp.concatenate(ia_rows, axis=0)
        tb_ref[rows, :] = jnp.concatenate(ib_rows, axis=0)
        tg_ref[rows, :] = jnp.concatenate(g_rows, axis=0)
        return carry

    lax.fori_loop(0, PEER_HEADS, head_body, 0)
    ia_ref[...] = ta_ref[...].T
    ib_ref[...] = tb_ref[...].T
    g_ref[...] = tg_ref[...].T


def _route(s1, s2):
    M = s1.shape[2]
    R = 512
    NS = PEER_HEADS * PEER_TOPK
    spec_in = pl.BlockSpec((PEER_HEADS, PEER_NKEYS, R), lambda i: (0, 0, i))
    spec_out = pl.BlockSpec((R, NS), lambda i: (i, 0))
    return pl.pallas_call(
        _route_kernel,
        grid=(M // R,),
        in_specs=[spec_in, spec_in],
        out_specs=[spec_out, spec_out, spec_out],
        out_shape=[jax.ShapeDtypeStruct((M, NS), F32)] * 3,
        scratch_shapes=[pltpu.VMEM((NS, R), F32)] * 3,
        compiler_params=_params(("parallel",)),
        name="route",
    )(s1, s2)


GM_PITCH = PEER_NKEYS // 2 + 4

def _peer_kernel(h2_ref, ia_ref, ib_ref, g_ref, u_ref, v_ref, x1_ref, g2_ref, fg_ref, o_ref,
                 gm_ref, acc_ref, *, T, AE, final_norm):
    e = pl.program_id(1)
    NK = PEER_NKEYS
    NA = NK // 2
    steps_per_build = NA // AE
    TOK_UNROLL = 32

    @pl.when(e == 0)
    def _():
        acc_ref[...] = jnp.zeros_like(acc_ref)

    @pl.when(e % steps_per_build == 0)
    def _():
        a_base = (e // steps_per_build) * NA
        sub_a = (lax.broadcasted_iota(I32, (NA, NK), 0) + a_base).astype(F32)
        sub_b = lax.broadcasted_iota(I32, (NK, NK), 0).astype(F32)

        zero = jnp.zeros((NA, NK), BF16)

        def tok_body(t8, carry):
            r0 = pl.multiple_of(t8 * TOK_UNROLL, TOK_UNROLL)
            base = pl.multiple_of(r0 * GM_PITCH, TOK_UNROLL)
            ia8 = ia_ref[pl.ds(r0, TOK_UNROLL), :]
            ib8 = ib_ref[pl.ds(r0, TOK_UNROLL), :]
            g8 = g_ref[pl.ds(r0, TOK_UNROLL), :]
            for j in range(0, TOK_UNROLL, 2):
                am = [jnp.where(sub_a == ia8[k:k + 1, :], g8[k:k + 1, :], 0.0).astype(BF16) for k in (j, j + 1)]
                bm = [jnp.where(sub_b == ib8[k:k + 1, :], 1.0, 0.0).astype(BF16) for k in (j, j + 1)]
                lhs = jnp.concatenate([jnp.concatenate([am[0], zero], axis=1),
                                       jnp.concatenate([zero, am[1]], axis=1)], axis=0)
                pair = _dot_nt(lhs, jnp.concatenate(bm, axis=1))
                gm_ref[pl.ds(base + j * GM_PITCH, NA), :] = pair[:NA]
                gm_ref[pl.ds(base + (j + 1) * GM_PITCH, NA), :] = pair[NA:]
            return carry

        lax.fori_loop(0, T // TOK_UNROLL, tok_body, 0)

    a_loc = (e % steps_per_build) * AE
    s = _dot_nt(h2_ref[...], u_ref[...])
    gt = jnp.concatenate([gm_ref[pl.ds(a_loc + j, T, stride=GM_PITCH), :] for j in range(AE)], axis=1)
    act = 0.5 * s * (1.0 + lax.erf(s * (0.5 ** 0.5)))
    acc_ref[...] += _dot((act * gt).astype(BF16), v_ref[...])

    @pl.when(e == pl.num_programs(1) - 1)
    def _():
        x2 = x1_ref[...] + g2_ref[0] * acc_ref[...]
        if final_norm:
            ms = jnp.mean(x2 * x2, axis=-1, keepdims=True)
            x2 = x2 * lax.rsqrt(ms + NORM_EPS) * fg_ref[...]
        o_ref[...] = x2


def _peer(h2, ia, ib, g, u, v, x1, ada3, final_g, S, final_norm):
    M, D = x1.shape
    T = min(512, S)
    AE = 8
    ET = AE * PEER_NKEYS
    NE = u.shape[0]
    NS = PEER_HEADS * PEER_TOPK
    tpb = S // T
    return pl.pallas_call(
        functools.partial(_peer_kernel, T=T, AE=AE, final_norm=final_norm),
        grid=(M // T, NE // ET),
        in_specs=[pl.BlockSpec((T, D), lambda i, e: (i, 0)),
                  pl.BlockSpec((T, NS), lambda i, e: (i, 0)),
                  pl.BlockSpec((T, NS), lambda i, e: (i, 0)),
                  pl.BlockSpec((T, NS), lambda i, e: (i, 0)),
                  pl.BlockSpec((ET, D), lambda i, e: (e, 0)),
                  pl.BlockSpec((ET, D), lambda i, e: (e, 0)),
                  pl.BlockSpec((T, D), lambda i, e: (i, 0)),
                  pl.BlockSpec((1, 1, D), lambda i, e: (i // tpb, 0, 5)),
                  pl.BlockSpec((1, D), lambda i, e: (0, 0))],
        out_specs=pl.BlockSpec((T, D), lambda i, e: (i, 0)),
        out_shape=jax.ShapeDtypeStruct((M, D), F32),
        scratch_shapes=[pltpu.VMEM((T * GM_PITCH, PEER_NKEYS), F32),
                        pltpu.VMEM((T, D), F32)],
        compiler_params=_params(("parallel", "arbitrary")),
        name="peer",
    )(h2, ia, ib, g, u, v, x1, ada3, final_g.reshape(1, D))


def _cat_w_in(w_in):
    sizes = (GLA_DK, GLA_DK, GLA_DV, GLA_DV, GLA_GATE_RANK, DSA_HEADS * DSA_HEAD_DIM, DSA_LATENT,
             IDX_HEADS * IDX_DIM, IDX_DIM, IDX_HEADS, D_MODEL, D_MODEL)
    offs = [0]
    for s in sizes:
        offs.append(offs[-1] + s)
    gq, gk, gv, gr, glow, dq, dkv, iq, ik, iw, za, zb = [w_in[:, offs[i]:offs[i + 1]] for i in range(len(sizes))]
    pad_misc = jnp.zeros((w_in.shape[0], 128 - (GLA_GATE_RANK + IDX_DIM + IDX_HEADS)), w_in.dtype)
    pad_end = jnp.zeros((w_in.shape[0], PROJ_COLS - (COL_MISC + 128)), w_in.dtype)
    return jnp.concatenate([gv, gr, dq, za, zb, gq, gk, iq, dkv, glow, ik, iw, pad_misc, pad_end], axis=1).astype(BF16)


def kernel(x, c, w_ada, b_ada, norm1_g, w_in, gla_w_gate_up, gla_b_gate, gla_norm_g, dsa_kv_norm_g, dsa_w_uk,
           dsa_w_uv, w_branch_a, w_branch_b, w_out, norm2_g, peer_w_q, peer_sub_keys_1, peer_sub_keys_2,
           peer_u, peer_v, final_norm_g):
    B, S, D = x.shape
    depth = w_ada.shape[0]
    x2 = x.reshape(B * S, D)
    for l in range(depth):
        ada3 = _ada(c, w_ada[l], b_ada[l]).reshape(B, 1, 6 * D)
        proj = _proj(x2, ada3, norm1_g[l], _cat_w_in(w_in[l]), S)
        ya = _gla(proj, gla_w_gate_up[l], gla_b_gate[l], gla_norm_g[l], B, S)
        ckv, ckvt, ikt = _kvprep(proj, dsa_kv_norm_g[l], _dsa_key_chunk(S))
        yb = _dsa(proj, ikt, ckv, ckvt, dsa_w_uk[l].astype(BF16), dsa_w_uv[l].astype(BF16), B, S)
        x1, h2, s1, s2 = _merge(ya, yb, proj, x2, ada3, norm2_g[l],
                                w_branch_a[l].astype(BF16), w_branch_b[l].astype(BF16), w_out[l].astype(BF16),
                                peer_w_q[l].astype(BF16), peer_sub_keys_1[l].astype(BF16),
                                peer_sub_keys_2[l].astype(BF16), S)
        ia, ib, g = _route(s1, s2)
        x2 = _peer(h2, ia, ib, g, peer_u[l].astype(BF16), peer_v[l].astype(BF16), x1, ada3, final_norm_g, S,
                   final_norm=(l == depth - 1))
    return x2.reshape(B, S, D)
```

```python
import functools

import jax
import jax.numpy as jnp
from jax import lax
from jax.experimental import pallas as pl
from jax.experimental.pallas import tpu as pltpu

F32 = jnp.float32
BF16 = jnp.bfloat16
I32 = jnp.int32

D_MODEL = 1024
GLA_HEADS = 4
GLA_DK = D_MODEL // 2
GLA_DV = D_MODEL
GLA_GATE_RANK = 16
GLA_TAU = 16.0
GLA_CHUNK = 64
DSA_HEADS = 8
DSA_HEAD_DIM = 128
DSA_LATENT = 256
IDX_HEADS = 8
IDX_DIM = 64
DSA_TOPK = 256
Q_BLOCK = 128
PEER_HEADS = 8
PEER_NKEYS = 128
PEER_DKEY = 256
PEER_TOPK = 16
NORM_EPS = 1e-6

INT_MIN = -(2 ** 31)
MASK_NEG = -1e30
MASK_DIST = 1e30

COL_GV, COL_GR, COL_DQ, COL_ZA, COL_ZB = 0, 1024, 2048, 3072, 4096
COL_GQ, COL_GK, COL_IQ, COL_DKV, COL_MISC = 5120, 5632, 6144, 6656, 6912
PROJ_COLS = 7168
MISC_GLOW, MISC_IK, MISC_IW = 0, 16, 80

VMEM_LIMIT = 56 * 1024 * 1024


def _params(sem):
    return pltpu.CompilerParams(dimension_semantics=sem, vmem_limit_bytes=VMEM_LIMIT)


def _dot(a, b):
    return jnp.dot(a, b, preferred_element_type=F32)


def _dot_nt(a, b):
    return lax.dot_general(a, b, (((1,), (1,)), ((), ())), preferred_element_type=F32)


def _dot_tn(a, b):
    return lax.dot_general(a, b, (((0,), (0,)), ((), ())), preferred_element_type=F32)


def _ada_kernel(c_ref, w_ref, b_ref, o_ref):
    c = c_ref[...]
    a = c * jax.nn.sigmoid(c)
    o_ref[...] = _dot(a.astype(BF16), w_ref[...].astype(BF16)) + b_ref[...]


def _ada(c, w_ada, b_ada):
    B, D = c.shape
    N = w_ada.shape[1]
    tn = 1024
    return pl.pallas_call(
        _ada_kernel,
        grid=(N // tn,),
        in_specs=[pl.BlockSpec((B, D), lambda j: (0, 0)),
                  pl.BlockSpec((D, tn), lambda j: (0, j)),
                  pl.BlockSpec((1, tn), lambda j: (0, j))],
        out_specs=pl.BlockSpec((B, tn), lambda j: (0, j)),
        out_shape=jax.ShapeDtypeStruct((B, N), F32),
        compiler_params=_params(("parallel",)),
        name="ada",
    )(c, w_ada, b_ada.reshape(1, N))


def _proj_kernel(x_ref, sc_ref, sh_ref, g_ref, w_ref, o_ref, h_ref):
    @pl.when(pl.program_id(1) == 0)
    def _():
        x = x_ref[...]
        ms = jnp.mean(x * x, axis=-1, keepdims=True)
        y = x * lax.rsqrt(ms + NORM_EPS) * g_ref[...]
        h_ref[...] = (y * (1.0 + sc_ref[0]) + sh_ref[0]).astype(BF16)

    o_ref[...] = _dot(h_ref[...], w_ref[...]).astype(BF16)


def _proj(x2, ada3, norm_g, w_cat, S):
    M, D = x2.shape
    N = w_cat.shape[1]
    tm, tn = min(2048, S), 512
    tpb = S // tm
    return pl.pallas_call(
        _proj_kernel,
        grid=(M // tm, N // tn),
        in_specs=[pl.BlockSpec((tm, D), lambda i, j: (i, 0)),
                  pl.BlockSpec((1, 1, D), lambda i, j: (i // tpb, 0, 1)),
                  pl.BlockSpec((1, 1, D), lambda i, j: (i // tpb, 0, 0)),
                  pl.BlockSpec((1, D), lambda i, j: (0, 0)),
                  pl.BlockSpec((D, tn), lambda i, j: (0, j))],
        out_specs=pl.BlockSpec((tm, tn), lambda i, j: (i, j)),
        out_shape=jax.ShapeDtypeStruct((M, N), BF16),
        scratch_shapes=[pltpu.VMEM((tm, D), BF16)],
        compiler_params=_params(("parallel", "arbitrary")),
        name="proj",
    )(x2, ada3, ada3, norm_g.reshape(1, D), w_cat)


def _gla_kernel(q_ref, k_ref, v_ref, r_ref, misc_ref, wg_ref, bg_ref, ng_ref, o_ref, st_ref, *, n_chunks):
    C = GLA_CHUNK
    H = GLA_HEADS
    dkh = GLA_DK // H
    dvh = GLA_DV // H

    @pl.when(pl.program_id(1) == 0)
    def _():
        st_ref[...] = jnp.zeros_like(st_ref)

    wg = wg_ref[...].astype(BF16)
    bg = bg_ref[...]
    ng = ng_ref[...]
    row = lax.broadcasted_iota(I32, (C, C), 0)
    col = lax.broadcasted_iota(I32, (C, C), 1)
    causal = row >= col
    tril = jnp.where(causal, 1.0, 0.0).astype(BF16)

    st = [st_ref[h] for h in range(H)]
    for c in range(n_chunks):
        sl = slice(c * C, (c + 1) * C)
        glow = misc_ref[sl, MISC_GLOW:MISC_GLOW + GLA_GATE_RANK]
        gate = _dot(glow, wg) + bg
        la = (jnp.minimum(gate, 0.0) - jnp.log1p(jnp.exp(-jnp.abs(gate)))) * (1.0 / GLA_TAU)
        la_hi = la.astype(BF16)
        la_lo = (la - la_hi.astype(F32)).astype(BF16)
        b = _dot(tril, la_hi) + _dot(tril, la_lo)
        b_last = b[C - 1:C, :]
        q_dec = (q_ref[sl, :].astype(F32) * (dkh ** -0.5) * jnp.exp(b)).astype(BF16)
        k = k_ref[sl, :].astype(F32)
        k_inv = (k * jnp.exp(-b)).astype(BF16)
        k_end = (k * jnp.exp(b_last - b)).astype(BF16)
        decay = jnp.exp(b_last)
        v = v_ref[sl, :]
        r = r_ref[sl, :].astype(F32)
        gate_r = r * jax.nn.sigmoid(r)
        for h in range(H):
            ks = slice(h * dkh, (h + 1) * dkh)
            vs = slice(h * dvh, (h + 1) * dvh)
            attn = jnp.where(causal, _dot_nt(q_dec[:, ks], k_inv[:, ks]), 0.0)
            o = _dot(attn.astype(BF16), v[:, vs]) + _dot_nt(q_dec[:, ks], st[h].astype(BF16))
            st[h] = decay[:, ks] * st[h] + _dot_tn(v[:, vs], k_end[:, ks])
            ms = jnp.mean(o * o, axis=-1, keepdims=True)
            on = o * lax.rsqrt(ms + NORM_EPS) * ng[:, vs]
            o_ref[sl, vs] = (on * gate_r[:, vs]).astype(BF16)
    for h in range(H):
        st_ref[h] = st[h]


def _gla(proj, w_gate_up, b_gate, norm_g, B, S):
    M = proj.shape[0]
    rb = min(256, S)
    nrb = S // rb
    dkh = GLA_DK // GLA_HEADS
    dvh = GLA_DV // GLA_HEADS
    rowmap = lambda b, c: b * nrb + c
    return pl.pallas_call(
        functools.partial(_gla_kernel, n_chunks=rb // GLA_CHUNK),
        grid=(B, nrb),
        in_specs=[pl.BlockSpec((rb, GLA_DK), lambda b, c: (rowmap(b, c), COL_GQ // GLA_DK)),
                  pl.BlockSpec((rb, GLA_DK), lambda b, c: (rowmap(b, c), COL_GK // GLA_DK)),
                  pl.BlockSpec((rb, GLA_DV), lambda b, c: (rowmap(b, c), COL_GV // GLA_DV)),
                  pl.BlockSpec((rb, GLA_DV), lambda b, c: (rowmap(b, c), COL_GR // GLA_DV)),
                  pl.BlockSpec((rb, 128), lambda b, c: (rowmap(b, c), COL_MISC // 128)),
                  pl.BlockSpec((GLA_GATE_RANK, GLA_DK), lambda b, c: (0, 0)),
                  pl.BlockSpec((1, GLA_DK), lambda b, c: (0, 0)),
                  pl.BlockSpec((1, GLA_DV), lambda b, c: (0, 0))],
        out_specs=pl.BlockSpec((rb, GLA_DV), lambda b, c: (rowmap(b, c), 0)),
        out_shape=jax.ShapeDtypeStruct((M, GLA_DV), BF16),
        scratch_shapes=[pltpu.VMEM((GLA_HEADS, dvh, dkh), F32)],
        compiler_params=_params(("parallel", "arbitrary")),
        name="gla",
    )(proj, proj, proj, proj, proj, w_gate_up, b_gate.reshape(1, GLA_DK), norm_g.reshape(1, GLA_DV))


def _kvprep_kernel(kv_ref, misc_ref, g_ref, ckv_ref, ckvt_ref, ikt_ref):
    x = kv_ref[...].astype(F32)
    ms = jnp.mean(x * x, axis=-1, keepdims=True)
    c = x * lax.rsqrt(ms + NORM_EPS) * g_ref[...]
    ckv_ref[...] = c.astype(BF16)
    ckvt_ref[0] = c.T.astype(BF16)
    ikt_ref[0] = misc_ref[:, MISC_IK:MISC_IK + IDX_DIM].astype(F32).T.astype(BF16)


def _kvprep(proj, kv_norm_g, KC):
    M = proj.shape[0]
    return pl.pallas_call(
        _kvprep_kernel,
        grid=(M // KC,),
        in_specs=[pl.BlockSpec((KC, DSA_LATENT), lambda i: (i, COL_DKV // DSA_LATENT)),
                  pl.BlockSpec((KC, 128), lambda i: (i, COL_MISC // 128)),
                  pl.BlockSpec((1, DSA_LATENT), lambda i: (0, 0))],
        out_specs=[pl.BlockSpec((KC, DSA_LATENT), lambda i: (i, 0)),
                   pl.BlockSpec((1, DSA_LATENT, KC), lambda i: (i, 0, 0)),
                   pl.BlockSpec((1, IDX_DIM, KC), lambda i: (i, 0, 0))],
        out_shape=[jax.ShapeDtypeStruct((M, DSA_LATENT), BF16),
                   jax.ShapeDtypeStruct((M // KC, DSA_LATENT, KC), BF16),
                   jax.ShapeDtypeStruct((M // KC, IDX_DIM, KC), BF16)],
        compiler_params=_params(("parallel",)),
        name="kvprep",
    )(proj, proj, kv_norm_g.reshape(1, DSA_LATENT))


def _dsa_kernel(dq_ref, iq_ref, misc_ref, ikt_ref, ckv_ref, ckvt_ref, wuk_ref, wuv_ref, o_ref,
                keys_ref, qlat_ref, m_ref, l_ref, acc_ref, thr_ref, tie_ref, *, S, KC, k_sel):
    QB = Q_BLOCK
    H = DSA_HEADS
    qb = pl.program_id(1)
    nkc = (qb * QB + QB + KC - 1) // KC
    t_pos = qb * QB + lax.broadcasted_iota(I32, (QB, 1), 0)
    lane_pos = lax.broadcasted_iota(I32, (1, KC), 1)

    iw = misc_ref[:, MISC_IW:MISC_IW + IDX_HEADS].astype(F32) * (IDX_HEADS ** -0.5) * (IDX_DIM ** -0.5)
    iq = iq_ref[...]

    def score_body(kc, carry):
        k0 = pl.multiple_of(kc * KC, KC)
        ikc = ikt_ref[kc]
        score = jnp.zeros((QB, KC), F32)
        for h in range(IDX_HEADS):
            rel = _dot(iq[:, h * IDX_DIM:(h + 1) * IDX_DIM], ikc)
            rel = jnp.maximum(rel, 0.0)
            score = score + iw[:, h:h + 1] * rel
        bits = pltpu.bitcast(score, I32)
        keyv = bits ^ ((bits >> 31) & 0x7FFFFFFF)
        valid = (k0 + lane_pos) <= t_pos
        keys_ref[kc] = jnp.where(valid, keyv, INT_MIN)
        return carry

    lax.fori_loop(0, nkc, score_body, 0)

    @pl.when(nkc % 2 == 1)
    def _():
        keys_ref[nkc] = jnp.full((QB, KC), INT_MIN, I32)

    def count(pred):
        def body(i, acc):
            for kc in (2 * i, 2 * i + 1):
                kv = keys_ref[kc]
                for g in range(KC // 128):
                    acc = acc + jnp.where(pred(kv[:, g * 128:(g + 1) * 128], kc * KC + g * 128), 1.0, 0.0)
            return acc
        acc = lax.fori_loop(0, (nkc + 1) // 2, body, jnp.zeros((QB, 128), F32))
        return jnp.sum(acc, axis=1, keepdims=True)

    def bit_body(i, thr):
        cand = thr + (jnp.int32(1) << (31 - i))
        cnt = count(lambda kv, base: kv >= cand)
        return jnp.where(cnt >= k_sel, cand, thr)

    thr = lax.fori_loop(0, 32, bit_body, jnp.full((QB, 1), INT_MIN, I32))
    thr_ref[...] = thr
    n_gt = count(lambda kv, base: kv > thr)
    n_ge = count(lambda kv, base: kv >= thr)
    tie_ref[...] = jnp.full((QB, 1), S, I32)

    @pl.when(jnp.max(n_ge) > k_sel)
    def _():
        need = k_sel - n_gt
        lane = lax.broadcasted_iota(I32, (1, 128), 1)

        def pos_body(i, p):
            cand = p + (jnp.int32(1) << (S.bit_length() - 1 - i))
            cnt = count(lambda kv, base: (kv == thr) & ((base + lane) < cand))
            return jnp.where(cnt < need, cand, p)

        tie_ref[...] = lax.fori_loop(0, S.bit_length(), pos_body, jnp.zeros((QB, 1), I32))

    thr = thr_ref[...]
    tie = tie_ref[...]

    for h in range(H):
        ql = _dot(dq_ref[:, h * DSA_HEAD_DIM:(h + 1) * DSA_HEAD_DIM], wuk_ref[h])
        qlat_ref[h * QB:(h + 1) * QB, :] = ql.astype(BF16)
    m_ref[...] = jnp.full(m_ref.shape, MASK_NEG, F32)
    l_ref[...] = jnp.zeros(l_ref.shape, F32)
    acc_ref[...] = jnp.zeros(acc_ref.shape, F32)

    log2e = 1.4426950408889634
    c_raw = (DSA_HEAD_DIM ** -0.5) * log2e
    HG = 2
    head = lax.broadcasted_iota(I32, (H * QB, 1), 0) // QB
    slope2 = jnp.exp2((head + 1).astype(F32) * (-8.0 / H)) * log2e

    def att_body(kc, carry):
        k0 = pl.multiple_of(kc * KC, KC)
        kv = ckv_ref[pl.ds(k0, KC), :]
        keyv = keys_ref[kc]
        kpos = k0 + lane_pos
        dist = t_pos - kpos
        sel = ((keyv > thr) | ((keyv == thr) & (kpos <= tie))) & (dist >= 0)
        dmask = jnp.where(sel, dist.astype(F32), MASK_DIST)
        dmask = jnp.concatenate([dmask] * HG, axis=0)
        kvt = ckvt_ref[kc]
        for g in range(H // HG):
            rows = slice(g * HG * QB, (g + 1) * HG * QB)
            lg = _dot(qlat_ref[rows], kvt) * c_raw - slope2[rows] * dmask
            m_old = m_ref[rows]
            m_new = jnp.maximum(m_old, jnp.max(lg, axis=1, keepdims=True))
            alpha = jnp.exp2(m_old - m_new)
            p = jnp.exp2(lg - m_new)
            l_ref[rows] = alpha * l_ref[rows] + jnp.sum(p, axis=1, keepdims=True)
            acc_ref[rows] = alpha * acc_ref[rows] + _dot(p.astype(BF16), kv)
            m_ref[rows] = m_new
        return carry

    lax.fori_loop(0, nkc, att_body, 0)

    for h in range(H):
        rows = slice(h * QB, (h + 1) * QB)
        o_lat = acc_ref[rows] / l_ref[rows]
        o_ref[:, h * DSA_HEAD_DIM:(h + 1) * DSA_HEAD_DIM] = _dot(o_lat.astype(BF16), wuv_ref[h]).astype(BF16)


def _dsa_key_chunk(S):
    return min(512, S)


def _dsa(proj, ikt, ckv, ckvt, w_uk, w_uv, B, S):
    M = proj.shape[0]
    QB = Q_BLOCK
    nqb = S // QB
    KC = _dsa_key_chunk(S)
    k_sel = min(DSA_TOPK, S // 4)
    HD = DSA_HEADS * DSA_HEAD_DIM
    return pl.pallas_call(
        functools.partial(_dsa_kernel, S=S, KC=KC, k_sel=k_sel),
        grid=(B, nqb),
        in_specs=[pl.BlockSpec((QB, HD), lambda b, q: (b * nqb + q, COL_DQ // HD)),
                  pl.BlockSpec((QB, IDX_HEADS * IDX_DIM), lambda b, q: (b * nqb + q, COL_IQ // (IDX_HEADS * IDX_DIM))),
                  pl.BlockSpec((QB, 128), lambda b, q: (b * nqb + q, COL_MISC // 128)),
                  pl.BlockSpec((S // KC, IDX_DIM, KC), lambda b, q: (b, 0, 0)),
                  pl.BlockSpec((S, DSA_LATENT), lambda b, q: (b, 0)),
                  pl.BlockSpec((S // KC, DSA_LATENT, KC), lambda b, q: (b, 0, 0)),
                  pl.BlockSpec((DSA_HEADS, DSA_HEAD_DIM, DSA_LATENT), lambda b, q: (0, 0, 0)),
                  pl.BlockSpec((DSA_HEADS, DSA_LATENT, DSA_HEAD_DIM), lambda b, q: (0, 0, 0))],
        out_specs=pl.BlockSpec((QB, HD), lambda b, q: (b * nqb + q, 0)),
        out_shape=jax.ShapeDtypeStruct((M, HD), BF16),
        scratch_shapes=[pltpu.VMEM((S // KC + 1, QB, KC), I32),
                        pltpu.VMEM((DSA_HEADS * QB, DSA_LATENT), BF16),
                        pltpu.VMEM((DSA_HEADS * QB, 1), F32),
                        pltpu.VMEM((DSA_HEADS * QB, 1), F32),
                        pltpu.VMEM((DSA_HEADS * QB, DSA_LATENT), F32),
                        pltpu.VMEM((QB, 1), I32),
                        pltpu.VMEM((QB, 1), I32)],
        compiler_params=_params(("parallel", "arbitrary")),
        name="dsa",
    )(proj, proj, proj, ikt, ckv, ckvt, w_uk, w_uv)


def _merge_kernel(ya_ref, yb_ref, za_ref, zb_ref, x_ref, g1_ref, sc2_ref, sh2_ref, n2_ref,
                  wa_ref, wb_ref, wo_ref, wq_ref, k1_ref, k2_ref,
                  x1_ref, h2_ref, s1_ref, s2_ref):
    ya = _dot(ya_ref[...], wa_ref[...])
    yb = _dot(yb_ref[...], wb_ref[...])
    mix = jax.nn.sigmoid(za_ref[...].astype(F32)) * ya + jax.nn.sigmoid(zb_ref[...].astype(F32)) * yb
    y = _dot(mix.astype(BF16), wo_ref[...])
    x1 = x_ref[...] + g1_ref[0] * y
    x1_ref[...] = x1
    ms = jnp.mean(x1 * x1, axis=-1, keepdims=True)
    h2 = (x1 * lax.rsqrt(ms + NORM_EPS) * n2_ref[...]) * (1.0 + sc2_ref[0]) + sh2_ref[0]
    h2b = h2.astype(BF16)
    h2_ref[...] = h2b
    q = _dot(h2b, wq_ref[...]).astype(BF16)
    half = PEER_DKEY // 2
    for h in range(PEER_HEADS):
        s1_ref[h] = _dot_nt(k1_ref[...], q[:, h * PEER_DKEY:h * PEER_DKEY + half])
        s2_ref[h] = _dot_nt(k2_ref[...], q[:, h * PEER_DKEY + half:(h + 1) * PEER_DKEY])


def _merge(ya, yb, proj, x2, ada3, norm2_g, wa, wb, wo, wq, k1, k2, S):
    M, D = x2.shape
    tm = min(512, S)
    tpb = S // tm
    full = lambda shape: pl.BlockSpec(shape, lambda i: (0,) * len(shape))
    adaspec = lambda k: pl.BlockSpec((1, 1, D), lambda i: (i // tpb, 0, k))
    return pl.pallas_call(
        _merge_kernel,
        grid=(M // tm,),
        in_specs=[pl.BlockSpec((tm, D), lambda i: (i, 0)),
                  pl.BlockSpec((tm, D), lambda i: (i, 0)),
                  pl.BlockSpec((tm, D), lambda i: (i, COL_ZA // D)),
                  pl.BlockSpec((tm, D), lambda i: (i, COL_ZB // D)),
                  pl.BlockSpec((tm, D), lambda i: (i, 0)),
                  adaspec(2), adaspec(4), adaspec(3),
                  full((1, D)),
                  full((D, D)), full((D, D)), full((D, D)), full((D, PEER_HEADS * PEER_DKEY)),
                  full((PEER_NKEYS, PEER_DKEY // 2)), full((PEER_NKEYS, PEER_DKEY // 2))],
        out_specs=[pl.BlockSpec((tm, D), lambda i: (i, 0)),
                   pl.BlockSpec((tm, D), lambda i: (i, 0)),
                   pl.BlockSpec((PEER_HEADS, PEER_NKEYS, tm), lambda i: (0, 0, i)),
                   pl.BlockSpec((PEER_HEADS, PEER_NKEYS, tm), lambda i: (0, 0, i))],
        out_shape=[jax.ShapeDtypeStruct((M, D), F32),
                   jax.ShapeDtypeStruct((M, D), BF16),
                   jax.ShapeDtypeStruct((PEER_HEADS, PEER_NKEYS, M), F32),
                   jax.ShapeDtypeStruct((PEER_HEADS, PEER_NKEYS, M), F32)],
        compiler_params=_params(("parallel",)),
        name="merge",
    )(ya, yb, proj, proj, x2, ada3, ada3, ada3, norm2_g.reshape(1, D), wa, wb, wo, wq, k1, k2)


_PAIRS = [(j1, j2) for j1 in range(PEER_TOPK) for j2 in range(PEER_TOPK) if (j1 + 1) * (j2 + 1) <= PEER_TOPK]
_NCAND = 1 << (len(_PAIRS) - 1).bit_length()


def _extract_top_tagged(s, rounds):
    n = s.shape[0]
    low = n - 1
    assert n & low == 0
    iota = lax.broadcasted_iota(I32, s.shape, 0)
    bits = pltpu.bitcast(s, I32)
    ordered = bits ^ ((bits >> 31) & 0x7FFFFFFF)
    keys = pltpu.bitcast((((ordered >> 1) + (1 << 30)) & ~low) | (low - iota), F32)
    tops = []
    for _ in range(rounds):
        m = jnp.max(keys, axis=0, keepdims=True)
        keys = jnp.where(keys == m, 0.0, keys)
        tops.append(m)
    vals, idxs = [], []
    for m in tops:
        mb = pltpu.bitcast(m, I32)
        ob = ((mb & ~low) - (1 << 30)) << 1
        vals.append(pltpu.bitcast(ob ^ ((ob >> 31) & 0x7FFFFFFF), F32))
        idxs.append((low - (mb & low)).astype(F32))
    return vals, idxs


def _route_kernel(s1_ref, s2_ref, ia_ref, ib_ref, g_ref, ta_ref, tb_ref, tg_ref):
    R = s1_ref.shape[2]
    K = PEER_TOPK
    pad = _NCAND - len(_PAIRS)

    def head_body(h, carry):
        v1, i1 = _extract_top_tagged(s1_ref[h], K)
        v2, i2 = _extract_top_tagged(s2_ref[h], K)
        cand = jnp.concatenate([v1[a] + v2[b] for a, b in _PAIRS] + [jnp.full((pad, R), MASK_NEG, F32)], axis=0)
        ca = jnp.concatenate([i1[a] for a, b in _PAIRS] + [jnp.zeros((pad, R), F32)], axis=0)
        cb = jnp.concatenate([i2[b] for a, b in _PAIRS] + [jnp.zeros((pad, R), F32)], axis=0)
        best, pos = _extract_top_tagged(cand, K)
        iota = lax.broadcasted_iota(I32, cand.shape, 0).astype(F32)
        e = [jnp.exp(bv - best[0]) for bv in best]
        denom = e[0]
        for j in range(1, K):
            denom = denom + e[j]
        ia_rows, ib_rows, g_rows = [], [], []
        for j in range(K):
            hit = iota == pos[j]
            ia_rows.append(jnp.sum(jnp.where(hit, ca, 0.0), axis=0, keepdims=True))
            ib_rows.append(jnp.sum(jnp.where(hit, cb, 0.0), axis=0, keepdims=True))
            g_rows.append(e[j] / denom)
        rows = pl.ds(pl.multiple_of(h * K, K), K)
        ta_ref[rows, :] = jnp.concatenate(ia_rows, axis=0)
        tb_ref[rows, :] = jnp.concatenate(ib_rows, axis=0)
        tg_ref[rows, :] = jnp.concatenate(g_rows, axis=0)
        return carry

    lax.fori_loop(0, PEER_HEADS, head_body, 0)
    ia_ref[...] = ta_ref[...].T
    ib_ref[...] = tb_ref[...].T
    g_ref[...] = tg_ref[...].T


def _route(s1, s2):
    M = s1.shape[2]
    R = 512
    NS = PEER_HEADS * PEER_TOPK
    spec_in = pl.BlockSpec((PEER_HEADS, PEER_NKEYS, R), lambda i: (0, 0, i))
    spec_out = pl.BlockSpec((R, NS), lambda i: (i, 0))
    return pl.pallas_call(
        _route_kernel,
        grid=(M // R,),
        in_specs=[spec_in, spec_in],
        out_specs=[spec_out, spec_out, spec_out],
        out_shape=[jax.ShapeDtypeStruct((M, NS), F32)] * 3,
        scratch_shapes=[pltpu.VMEM((NS, R), F32)] * 3,
        compiler_params=_params(("parallel",)),
        name="route",
    )(s1, s2)


GM_PITCH = PEER_NKEYS // 2 + 4

def _peer_kernel(h2_ref, ia_ref, ib_ref, g_ref, u_ref, v_ref, x1_ref, g2_ref, fg_ref, o_ref,
                 gm_ref, acc_ref, *, T, AE, final_norm):
    e = pl.program_id(1)
    NK = PEER_NKEYS
    NA = NK // 2
    steps_per_build = NA // AE
    TOK_UNROLL = 32

    @pl.when(e == 0)
    def _():
        acc_ref[...] = jnp.zeros_like(acc_ref)

    @pl.when(e % steps_per_build == 0)
    def _():
        a_base = (e // steps_per_build) * NA
        sub_a = (lax.broadcasted_iota(I32, (NA, NK), 0) + a_base).astype(F32)
        sub_b = lax.broadcasted_iota(I32, (NK, NK), 0).astype(F32)

        zero = jnp.zeros((NA, NK), BF16)

        def tok_body(t8, carry):
            r0 = pl.multiple_of(t8 * TOK_UNROLL, TOK_UNROLL)
            base = pl.multiple_of(r0 * GM_PITCH, TOK_UNROLL)
            ia8 = ia_ref[pl.ds(r0, TOK_UNROLL), :]
            ib8 = ib_ref[pl.ds(r0, TOK_UNROLL), :]
            g8 = g_ref[pl.ds(r0, TOK_UNROLL), :]
            for j in range(0, TOK_UNROLL, 2):
                am = [jnp.where(sub_a == ia8[k:k + 1, :], g8[k:k + 1, :], 0.0).astype(BF16) for k in (j, j + 1)]
                bm = [jnp.where(sub_b == ib8[k:k + 1, :], 1.0, 0.0).astype(BF16) for k in (j, j + 1)]
                lhs = jnp.concatenate([jnp.concatenate([am[0], zero], axis=1),
                                       jnp.concatenate([zero, am[1]], axis=1)], axis=0)
                pair = _dot_nt(lhs, jnp.concatenate(bm, axis=1))
                gm_ref[pl.ds(base + j * GM_PITCH, NA), :] = pair[:NA]
                gm_ref[pl.ds(base + (j + 1) * GM_PITCH, NA), :] = pair[NA:]
            return carry

        lax.fori_loop(0, T // TOK_UNROLL, tok_body, 0)

    a_loc = (e % steps_per_build) * AE
    s = _dot_nt(h2_ref[...], u_ref[...])
    gt = jnp.concatenate([gm_ref[pl.ds(a_loc + j, T, stride=GM_PITCH), :] for j in range(AE)], axis=1)
    act = 0.5 * s * (1.0 + lax.erf(s * (0.5 ** 0.5)))
    acc_ref[...] += _dot((act * gt).astype(BF16), v_ref[...])

    @pl.when(e == pl.num_programs(1) - 1)
    def _():
        x2 = x1_ref[...] + g2_ref[0] * acc_ref[...]
        if final_norm:
            ms = jnp.mean(x2 * x2, axis=-1, keepdims=True)
            x2 = x2 * lax.rsqrt(ms + NORM_EPS) * fg_ref[...]
        o_ref[...] = x2


def _peer(h2, ia, ib, g, u, v, x1, ada3, final_g, S, final_norm):
    M, D = x1.shape
    T = min(512, S)
    AE = 8
    ET = AE * PEER_NKEYS
    NE = u.shape[0]
    NS = PEER_HEADS * PEER_TOPK
    tpb = S // T
    return pl.pallas_call(
        functools.partial(_peer_kernel, T=T, AE=AE, final_norm=final_norm),
        grid=(M // T, NE // ET),
        in_specs=[pl.BlockSpec((T, D), lambda i, e: (i, 0)),
                  pl.BlockSpec((T, NS), lambda i, e: (i, 0)),
                  pl.BlockSpec((T, NS), lambda i, e: (i, 0)),
                  pl.BlockSpec((T, NS), lambda i, e: (i, 0)),
                  pl.BlockSpec((ET, D), lambda i, e: (e, 0)),
                  pl.BlockSpec((ET, D), lambda i, e: (e, 0)),
                  pl.BlockSpec((T, D), lambda i, e: (i, 0)),
                  pl.BlockSpec((1, 1, D), lambda i, e: (i // tpb, 0, 5)),
                  pl.BlockSpec((1, D), lambda i, e: (0, 0))],
        out_specs=pl.BlockSpec((T, D), lambda i, e: (i, 0)),
        out_shape=jax.ShapeDtypeStruct((M, D), F32),
        scratch_shapes=[pltpu.VMEM((T * GM_PITCH, PEER_NKEYS), F32),
                        pltpu.VMEM((T, D), F32)],
        compiler_params=_params(("parallel", "arbitrary")),
        name="peer",
    )(h2, ia, ib, g, u, v, x1, ada3, final_g.reshape(1, D))


def _cat_w_in(w_in):
    sizes = (GLA_DK, GLA_DK, GLA_DV, GLA_DV, GLA_GATE_RANK, DSA_HEADS * DSA_HEAD_DIM, DSA_LATENT,
             IDX_HEADS * IDX_DIM, IDX_DIM, IDX_HEADS, D_MODEL, D_MODEL)
    offs = [0]
    for s in sizes:
        offs.append(offs[-1] + s)
    gq, gk, gv, gr, glow, dq, dkv, iq, ik, iw, za, zb = [w_in[:, offs[i]:offs[i + 1]] for i in range(len(sizes))]
    pad_misc = jnp.zeros((w_in.shape[0], 128 - (GLA_GATE_RANK + IDX_DIM + IDX_HEADS)), w_in.dtype)
    pad_end = jnp.zeros((w_in.shape[0], PROJ_COLS - (COL_MISC + 128)), w_in.dtype)
    return jnp.concatenate([gv, gr, dq, za, zb, gq, gk, iq, dkv, glow, ik, iw, pad_misc, pad_end], axis=1).astype(BF16)


def kernel(x, c, w_ada, b_ada, norm1_g, w_in, gla_w_gate_up, gla_b_gate, gla_norm_g, dsa_kv_norm_g, dsa_w_uk,
           dsa_w_uv, w_branch_a, w_branch_b, w_out, norm2_g, peer_w_q, peer_sub_keys_1, peer_sub_keys_2,
           peer_u, peer_v, final_norm_g):
    B, S, D = x.shape
    depth = w_ada.shape[0]
    x2 = x.reshape(B * S, D)
    for l in range(depth):
        ada3 = _ada(c, w_ada[l], b_ada[l]).reshape(B, 1, 6 * D)
        proj = _proj(x2, ada3, norm1_g[l], _cat_w_in(w_in[l]), S)
        ya = _gla(proj, gla_w_gate_up[l], gla_b_gate[l], gla_norm_g[l], B, S)
        ckv, ckvt, ikt = _kvprep(proj, dsa_kv_norm_g[l], _dsa_key_chunk(S))
        yb = _dsa(proj, ikt, ckv, ckvt, dsa_w_uk[l].astype(BF16), dsa_w_uv[l].astype(BF16), B, S)
        x1, h2, s1, s2 = _merge(ya, yb, proj, x2, ada3, norm2_g[l],
                                w_branch_a[l].astype(BF16), w_branch_b[l].astype(BF16), w_out[l].astype(BF16),
                                peer_w_q[l].astype(BF16), peer_sub_keys_1[l].astype(BF16),
                                peer_sub_keys_2[l].astype(BF16), S)
        ia, ib, g = _route(s1, s2)
        x2 = _peer(h2, ia, ib, g, peer_u[l].astype(BF16), peer_v[l].astype(BF16), x1, ada3, final_norm_g, S,
                   final_norm=(l == depth - 1))
    return x2.reshape(B, S, D)
```

```python
import functools

import jax
import jax.numpy as jnp
from jax import lax
from jax.experimental import pallas as pl
from jax.experimental.pallas import tpu as pltpu

F32 = jnp.float32
BF16 = jnp.bfloat16
I32 = jnp.int32

D_MODEL = 1024
GLA_HEADS = 4
GLA_DK = D_MODEL // 2
GLA_DV = D_MODEL
GLA_GATE_RANK = 16
GLA_TAU = 16.0
GLA_CHUNK = 64
DSA_HEADS = 8
DSA_HEAD_DIM = 128
DSA_LATENT = 256
IDX_HEADS = 8
IDX_DIM = 64
DSA_TOPK = 256
Q_BLOCK = 128
PEER_HEADS = 8
PEER_NKEYS = 128
PEER_DKEY = 256
PEER_TOPK = 16
NORM_EPS = 1e-6

INT_MIN = -(2 ** 31)
MASK_NEG = -1e30
MASK_DIST = 1e30

COL_GV, COL_GR, COL_DQ, COL_ZA, COL_ZB = 0, 1024, 2048, 3072, 4096
COL_GQ, COL_GK, COL_IQ, COL_DKV, COL_MISC = 5120, 5632, 6144, 6656, 6912
PROJ_COLS = 7168
MISC_GLOW, MISC_IK, MISC_IW = 0, 16, 80

VMEM_LIMIT = 56 * 1024 * 1024


def _params(sem):
    return pltpu.CompilerParams(dimension_semantics=sem, vmem_limit_bytes=VMEM_LIMIT)


def _dot(a, b):
    return jnp.dot(a, b, preferred_element_type=F32)


def _dot_nt(a, b):
    return lax.dot_general(a, b, (((1,), (1,)), ((), ())), preferred_element_type=F32)


def _dot_tn(a, b):
    return lax.dot_general(a, b, (((0,), (0,)), ((), ())), preferred_element_type=F32)


def _ada_kernel(c_ref, w_ref, b_ref, o_ref):
    c = c_ref[...]
    a = c * jax.nn.sigmoid(c)
    o_ref[...] = _dot(a.astype(BF16), w_ref[...].astype(BF16)) + b_ref[...]


def _ada(c, w_ada, b_ada):
    B, D = c.shape
    N = w_ada.shape[1]
    tn = 1024
    return pl.pallas_call(
        _ada_kernel,
        grid=(N // tn,),
        in_specs=[pl.BlockSpec((B, D), lambda j: (0, 0)),
                  pl.BlockSpec((D, tn), lambda j: (0, j)),
                  pl.BlockSpec((1, tn), lambda j: (0, j))],
        out_specs=pl.BlockSpec((B, tn), lambda j: (0, j)),
        out_shape=jax.ShapeDtypeStruct((B, N), F32),
        compiler_params=_params(("parallel",)),
        name="ada",
    )(c, w_ada, b_ada.reshape(1, N))


def _proj_kernel(x_ref, sc_ref, sh_ref, g_ref, w_ref, o_ref, h_ref):
    @pl.when(pl.program_id(1) == 0)
    def _():
        x = x_ref[...]
        ms = jnp.mean(x * x, axis=-1, keepdims=True)
        y = x * lax.rsqrt(ms + NORM_EPS) * g_ref[...]
        h_ref[...] = (y * (1.0 + sc_ref[0]) + sh_ref[0]).astype(BF16)

    o_ref[...] = _dot(h_ref[...], w_ref[...]).astype(BF16)


def _proj(x2, ada3, norm_g, w_cat, S):
    M, D = x2.shape
    N = w_cat.shape[1]
    tm, tn = min(2048, S), 512
    tpb = S // tm
    return pl.pallas_call(
        _proj_kernel,
        grid=(M // tm, N // tn),
        in_specs=[pl.BlockSpec((tm, D), lambda i, j: (i, 0)),
                  pl.BlockSpec((1, 1, D), lambda i, j: (i // tpb, 0, 1)),
                  pl.BlockSpec((1, 1, D), lambda i, j: (i // tpb, 0, 0)),
                  pl.BlockSpec((1, D), lambda i, j: (0, 0)),
                  pl.BlockSpec((D, tn), lambda i, j: (0, j))],
        out_specs=pl.BlockSpec((tm, tn), lambda i, j: (i, j)),
        out_shape=jax.ShapeDtypeStruct((M, N), BF16),
        scratch_shapes=[pltpu.VMEM((tm, D), BF16)],
        compiler_params=_params(("parallel", "arbitrary")),
        name="proj",
    )(x2, ada3, ada3, norm_g.reshape(1, D), w_cat)


def _gla_kernel(q_ref, k_ref, v_ref, r_ref, misc_ref, wg_ref, bg_ref, ng_ref, o_ref, st_ref, *, n_chunks):
    C = GLA_CHUNK
    H = GLA_HEADS
    dkh = GLA_DK // H
    dvh = GLA_DV // H

    @pl.when(pl.program_id(1) == 0)
    def _():
        st_ref[...] = jnp.zeros_like(st_ref)

    wg = wg_ref[...].astype(BF16)
    bg = bg_ref[...]
    ng = ng_ref[...]
    row = lax.broadcasted_iota(I32, (C, C), 0)
    col = lax.broadcasted_iota(I32, (C, C), 1)
    causal = row >= col
    tril = jnp.where(causal, 1.0, 0.0).astype(BF16)

    st = [st_ref[h] for h in range(H)]
    for c in range(n_chunks):
        sl = slice(c * C, (c + 1) * C)
        glow = misc_ref[sl, MISC_GLOW:MISC_GLOW + GLA_GATE_RANK]
        gate = _dot(glow, wg) + bg
        la = (jnp.minimum(gate, 0.0) - jnp.log1p(jnp.exp(-jnp.abs(gate)))) * (1.0 / GLA_TAU)
        la_hi = la.astype(BF16)
        la_lo = (la - la_hi.astype(F32)).astype(BF16)
        b = _dot(tril, la_hi) + _dot(tril, la_lo)
        b_last = b[C - 1:C, :]
        q_dec = (q_ref[sl, :].astype(F32) * (dkh ** -0.5) * jnp.exp(b)).astype(BF16)
        k = k_ref[sl, :].astype(F32)
        k_inv = (k * jnp.exp(-b)).astype(BF16)
        k_end = (k * jnp.exp(b_last - b)).astype(BF16)
        decay = jnp.exp(b_last)
        v = v_ref[sl, :]
        r = r_ref[sl, :].astype(F32)
        gate_r = r * jax.nn.sigmoid(r)
        for h in range(H):
            ks = slice(h * dkh, (h + 1) * dkh)
            vs = slice(h * dvh, (h + 1) * dvh)
            attn = jnp.where(causal, _dot_nt(q_dec[:, ks], k_inv[:, ks]), 0.0)
            o = _dot(attn.astype(BF16), v[:, vs]) + _dot_nt(q_dec[:, ks], st[h].astype(BF16))
            st[h] = decay[:, ks] * st[h] + _dot_tn(v[:, vs], k_end[:, ks])
            ms = jnp.mean(o * o, axis=-1, keepdims=True)
            on = o * lax.rsqrt(ms + NORM_EPS) * ng[:, vs]
            o_ref[sl, vs] = (on * gate_r[:, vs]).astype(BF16)
    for h in range(H):
        st_ref[h] = st[h]


def _gla(proj, w_gate_up, b_gate, norm_g, B, S):
    M = proj.shape[0]
    rb = min(256, S)
    nrb = S // rb
    dkh = GLA_DK // GLA_HEADS
    dvh = GLA_DV // GLA_HEADS
    rowmap = lambda b, c: b * nrb + c
    return pl.pallas_call(
        functools.partial(_gla_kernel, n_chunks=rb // GLA_CHUNK),
        grid=(B, nrb),
        in_specs=[pl.BlockSpec((rb, GLA_DK), lambda b, c: (rowmap(b, c), COL_GQ // GLA_DK)),
                  pl.BlockSpec((rb, GLA_DK), lambda b, c: (rowmap(b, c), COL_GK // GLA_DK)),
                  pl.BlockSpec((rb, GLA_DV), lambda b, c: (rowmap(b, c), COL_GV // GLA_DV)),
                  pl.BlockSpec((rb, GLA_DV), lambda b, c: (rowmap(b, c), COL_GR // GLA_DV)),
                  pl.BlockSpec((rb, 128), lambda b, c: (rowmap(b, c), COL_MISC // 128)),
                  pl.BlockSpec((GLA_GATE_RANK, GLA_DK), lambda b, c: (0, 0)),
                  pl.BlockSpec((1, GLA_DK), lambda b, c: (0, 0)),
                  pl.BlockSpec((1, GLA_DV), lambda b, c: (0, 0))],
        out_specs=pl.BlockSpec((rb, GLA_DV), lambda b, c: (rowmap(b, c), 0)),
        out_shape=jax.ShapeDtypeStruct((M, GLA_DV), BF16),
        scratch_shapes=[pltpu.VMEM((GLA_HEADS, dvh, dkh), F32)],
        compiler_params=_params(("parallel", "arbitrary")),
        name="gla",
    )(proj, proj, proj, proj, proj, w_gate_up, b_gate.reshape(1, GLA_DK), norm_g.reshape(1, GLA_DV))


def _kvprep_kernel(kv_ref, misc_ref, g_ref, ckv_ref, ckvt_ref, ikt_ref):
    x = kv_ref[...].astype(F32)
    ms = jnp.mean(x * x, axis=-1, keepdims=True)
    c = x * lax.rsqrt(ms + NORM_EPS) * g_ref[...]
    ckv_ref[...] = c.astype(BF16)
    ckvt_ref[0] = c.T.astype(BF16)
    ikt_ref[0] = misc_ref[:, MISC_IK:MISC_IK + IDX_DIM].astype(F32).T.astype(BF16)


def _kvprep(proj, kv_norm_g, KC):
    M = proj.shape[0]
    return pl.pallas_call(
        _kvprep_kernel,
        grid=(M // KC,),
        in_specs=[pl.BlockSpec((KC, DSA_LATENT), lambda i: (i, COL_DKV // DSA_LATENT)),
                  pl.BlockSpec((KC, 128), lambda i: (i, COL_MISC // 128)),
                  pl.BlockSpec((1, DSA_LATENT), lambda i: (0, 0))],
        out_specs=[pl.BlockSpec((KC, DSA_LATENT), lambda i: (i, 0)),
                   pl.BlockSpec((1, DSA_LATENT, KC), lambda i: (i, 0, 0)),
                   pl.BlockSpec((1, IDX_DIM, KC), lambda i: (i, 0, 0))],
        out_shape=[jax.ShapeDtypeStruct((M, DSA_LATENT), BF16),
                   jax.ShapeDtypeStruct((M // KC, DSA_LATENT, KC), BF16),
                   jax.ShapeDtypeStruct((M // KC, IDX_DIM, KC), BF16)],
        compiler_params=_params(("parallel",)),
        name="kvprep",
    )(proj, proj, kv_norm_g.reshape(1, DSA_LATENT))


def _dsa_kernel(dq_ref, iq_ref, misc_ref, ikt_ref, ckv_ref, ckvt_ref, wuk_ref, wuv_ref, o_ref,
                keys_ref, qlat_ref, m_ref, l_ref, acc_ref, thr_ref, tie_ref, *, S, KC, k_sel):
    QB = Q_BLOCK
    H = DSA_HEADS
    qb = pl.program_id(1)
    nkc = (qb * QB + QB + KC - 1) // KC
    t_pos = qb * QB + lax.broadcasted_iota(I32, (QB, 1), 0)
    lane_pos = lax.broadcasted_iota(I32, (1, KC), 1)

    iw = misc_ref[:, MISC_IW:MISC_IW + IDX_HEADS].astype(F32) * (IDX_HEADS ** -0.5) * (IDX_DIM ** -0.5)
    iq = iq_ref[...]

    def score_body(kc, carry):
        k0 = pl.multiple_of(kc * KC, KC)
        ikc = ikt_ref[kc]
        score = jnp.zeros((QB, KC), F32)
        for h in range(IDX_HEADS):
            rel = _dot(iq[:, h * IDX_DIM:(h + 1) * IDX_DIM], ikc)
            rel = jnp.maximum(rel, 0.0)
            score = score + iw[:, h:h + 1] * rel
        valid = (k0 + lane_pos) <= t_pos
        keys_ref[kc] = jnp.where(valid, score, -jnp.inf)
        return carry

    lax.fori_loop(0, nkc, score_body, 0)

    def count(pred):
        def body(kc, acc):
            kv = keys_ref[kc]
            for g in range(KC // 128):
                acc = acc + jnp.where(pred(kv[:, g * 128:(g + 1) * 128], kc * KC + g * 128), 1.0, 0.0)
            return acc
        acc = lax.fori_loop(0, nkc, body, jnp.zeros((QB, 128), F32))
        return jnp.sum(acc, axis=1, keepdims=True)

    def as_float(code):
        return pltpu.bitcast(code ^ ((code >> 31) & 0x7FFFFFFF), F32)

    def bit_body(i, code):
        cand = code + (jnp.int32(1) << (31 - i))
        cand_f = as_float(cand)
        cnt = count(lambda kv, base: kv >= cand_f)
        return jnp.where(cnt >= k_sel, cand, code)

    code = lax.fori_loop(0, 32, bit_body, jnp.full((QB, 1), INT_MIN, I32))
    thr = jnp.where(code == INT_MIN, -jnp.inf, as_float(code))
    thr_ref[...] = thr
    n_gt = count(lambda kv, base: kv > thr)
    n_ge = count(lambda kv, base: kv >= thr)
    tie_ref[...] = jnp.full((QB, 1), S, I32)

    @pl.when(jnp.max(n_ge) > k_sel)
    def _():
        need = k_sel - n_gt
        lane = lax.broadcasted_iota(I32, (1, 128), 1)

        def pos_body(i, p):
            cand = p + (jnp.int32(1) << (S.bit_length() - 1 - i))
            cnt = count(lambda kv, base: (kv == thr) & ((base + lane) < cand))
            return jnp.where(cnt < need, cand, p)

        tie_ref[...] = lax.fori_loop(0, S.bit_length(), pos_body, jnp.zeros((QB, 1), I32))

    thr = thr_ref[...]
    tie = tie_ref[...]

    for h in range(H):
        ql = _dot(dq_ref[:, h * DSA_HEAD_DIM:(h + 1) * DSA_HEAD_DIM], wuk_ref[h])
        qlat_ref[h * QB:(h + 1) * QB, :] = ql.astype(BF16)
    m_ref[...] = jnp.full(m_ref.shape, MASK_NEG, F32)
    l_ref[...] = jnp.zeros(l_ref.shape, F32)
    acc_ref[...] = jnp.zeros(acc_ref.shape, F32)

    log2e = 1.4426950408889634
    c_raw = (DSA_HEAD_DIM ** -0.5) * log2e
    HG = 2
    head = lax.broadcasted_iota(I32, (H * QB, 1), 0) // QB
    slope2 = jnp.exp2((head + 1).astype(F32) * (-8.0 / H)) * log2e

    def att_body(kc, carry):
        k0 = pl.multiple_of(kc * KC, KC)
        kv = ckv_ref[pl.ds(k0, KC), :]
        keyv = keys_ref[kc]
        kpos = k0 + lane_pos
        dist = t_pos - kpos
        sel = ((keyv > thr) | ((keyv == thr) & (kpos <= tie))) & (dist >= 0)
        dmask = jnp.where(sel, dist.astype(F32), MASK_DIST)
        dmask = jnp.concatenate([dmask] * HG, axis=0)
        kvt = ckvt_ref[kc]
        for g in range(H // HG):
            rows = slice(g * HG * QB, (g + 1) * HG * QB)
            lg = _dot(qlat_ref[rows], kvt) * c_raw - slope2[rows] * dmask
            m_old = m_ref[rows]
            m_new = jnp.maximum(m_old, jnp.max(lg, axis=1, keepdims=True))
            alpha = jnp.exp2(m_old - m_new)
            p = jnp.exp2(lg - m_new)
            l_ref[rows] = alpha * l_ref[rows] + jnp.sum(p, axis=1, keepdims=True)
            acc_ref[rows] = alpha * acc_ref[rows] + _dot(p.astype(BF16), kv)
            m_ref[rows] = m_new
        return carry

    lax.fori_loop(0, nkc, att_body, 0)

    for h in range(H):
        rows = slice(h * QB, (h + 1) * QB)
        o_lat = acc_ref[rows] / l_ref[rows]
        o_ref[:, h * DSA_HEAD_DIM:(h + 1) * DSA_HEAD_DIM] = _dot(o_lat.astype(BF16), wuv_ref[h]).astype(BF16)


def _dsa_key_chunk(S):
    return min(512, S)


def _dsa(proj, ikt, ckv, ckvt, w_uk, w_uv, B, S):
    M = proj.shape[0]
    QB = Q_BLOCK
    nqb = S // QB
    KC = _dsa_key_chunk(S)
    k_sel = min(DSA_TOPK, S // 4)
    HD = DSA_HEADS * DSA_HEAD_DIM
    return pl.pallas_call(
        functools.partial(_dsa_kernel, S=S, KC=KC, k_sel=k_sel),
        grid=(B, nqb),
        in_specs=[pl.BlockSpec((QB, HD), lambda b, q: (b * nqb + q, COL_DQ // HD)),
                  pl.BlockSpec((QB, IDX_HEADS * IDX_DIM), lambda b, q: (b * nqb + q, COL_IQ // (IDX_HEADS * IDX_DIM))),
                  pl.BlockSpec((QB, 128), lambda b, q: (b * nqb + q, COL_MISC // 128)),
                  pl.BlockSpec((S // KC, IDX_DIM, KC), lambda b, q: (b, 0, 0)),
                  pl.BlockSpec((S, DSA_LATENT), lambda b, q: (b, 0)),
                  pl.BlockSpec((S // KC, DSA_LATENT, KC), lambda b, q: (b, 0, 0)),
                  pl.BlockSpec((DSA_HEADS, DSA_HEAD_DIM, DSA_LATENT), lambda b, q: (0, 0, 0)),
                  pl.BlockSpec((DSA_HEADS, DSA_LATENT, DSA_HEAD_DIM), lambda b, q: (0, 0, 0))],
        out_specs=pl.BlockSpec((QB, HD), lambda b, q: (b * nqb + q, 0)),
        out_shape=jax.ShapeDtypeStruct((M, HD), BF16),
        scratch_shapes=[pltpu.VMEM((S // KC, QB, KC), F32),
                        pltpu.VMEM((DSA_HEADS * QB, DSA_LATENT), BF16),
                        pltpu.VMEM((DSA_HEADS * QB, 1), F32),
                        pltpu.VMEM((DSA_HEADS * QB, 1), F32),
                        pltpu.VMEM((DSA_HEADS * QB, DSA_LATENT), F32),
                        pltpu.VMEM((QB, 1), F32),
                        pltpu.VMEM((QB, 1), I32)],
        compiler_params=_params(("parallel", "arbitrary")),
        name="dsa",
    )(proj, proj, proj, ikt, ckv, ckvt, w_uk, w_uv)


def _merge_kernel(ya_ref, yb_ref, za_ref, zb_ref, x_ref, g1_ref, sc2_ref, sh2_ref, n2_ref,
                  wa_ref, wb_ref, wo_ref, wq_ref, k1_ref, k2_ref,
                  x1_ref, h2_ref, s1_ref, s2_ref):
    ya = _dot(ya_ref[...], wa_ref[...])
    yb = _dot(yb_ref[...], wb_ref[...])
    mix = jax.nn.sigmoid(za_ref[...].astype(F32)) * ya + jax.nn.sigmoid(zb_ref[...].astype(F32)) * yb
    y = _dot(mix.astype(BF16), wo_ref[...])
    x1 = x_ref[...] + g1_ref[0] * y
    x1_ref[...] = x1
    ms = jnp.mean(x1 * x1, axis=-1, keepdims=True)
    h2 = (x1 * lax.rsqrt(ms + NORM_EPS) * n2_ref[...]) * (1.0 + sc2_ref[0]) + sh2_ref[0]
    h2b = h2.astype(BF16)
    h2_ref[...] = h2b
    q = _dot(h2b, wq_ref[...]).astype(BF16)
    half = PEER_DKEY // 2
    for h in range(PEER_HEADS):
        s1_ref[h] = _dot_nt(k1_ref[...], q[:, h * PEER_DKEY:h * PEER_DKEY + half])
        s2_ref[h] = _dot_nt(k2_ref[...], q[:, h * PEER_DKEY + half:(h + 1) * PEER_DKEY])


def _merge(ya, yb, proj, x2, ada3, norm2_g, wa, wb, wo, wq, k1, k2, S):
    M, D = x2.shape
    tm = min(512, S)
    tpb = S // tm
    full = lambda shape: pl.BlockSpec(shape, lambda i: (0,) * len(shape))
    adaspec = lambda k: pl.BlockSpec((1, 1, D), lambda i: (i // tpb, 0, k))
    return pl.pallas_call(
        _merge_kernel,
        grid=(M // tm,),
        in_specs=[pl.BlockSpec((tm, D), lambda i: (i, 0)),
                  pl.BlockSpec((tm, D), lambda i: (i, 0)),
                  pl.BlockSpec((tm, D), lambda i: (i, COL_ZA // D)),
                  pl.BlockSpec((tm, D), lambda i: (i, COL_ZB // D)),
                  pl.BlockSpec((tm, D), lambda i: (i, 0)),
                  adaspec(2), adaspec(4), adaspec(3),
                  full((1, D)),
                  full((D, D)), full((D, D)), full((D, D)), full((D, PEER_HEADS * PEER_DKEY)),
                  full((PEER_NKEYS, PEER_DKEY // 2)), full((PEER_NKEYS, PEER_DKEY // 2))],
        out_specs=[pl.BlockSpec((tm, D), lambda i: (i, 0)),
                   pl.BlockSpec((tm, D), lambda i: (i, 0)),
                   pl.BlockSpec((PEER_HEADS, PEER_NKEYS, tm), lambda i: (0, 0, i)),
                   pl.BlockSpec((PEER_HEADS, PEER_NKEYS, tm), lambda i: (0, 0, i))],
        out_shape=[jax.ShapeDtypeStruct((M, D), F32),
                   jax.ShapeDtypeStruct((M, D), BF16),
                   jax.ShapeDtypeStruct((PEER_HEADS, PEER_NKEYS, M), F32),
                   jax.ShapeDtypeStruct((PEER_HEADS, PEER_NKEYS, M), F32)],
        compiler_params=_params(("parallel",)),
        name="merge",
    )(ya, yb, proj, proj, x2, ada3, ada3, ada3, norm2_g.reshape(1, D), wa, wb, wo, wq, k1, k2)


_PAIRS = [(j1, j2) for j1 in range(PEER_TOPK) for j2 in range(PEER_TOPK) if (j1 + 1) * (j2 + 1) <= PEER_TOPK]
_NCAND = -(-len(_PAIRS) // 8) * 8


def _extract_top(s, rounds):
    neg_iota = (-lax.broadcasted_iota(I32, s.shape, 0)).astype(F32)
    vals, idxs = [], []
    for _ in range(rounds):
        m = jnp.max(s, axis=0, keepdims=True)
        neg_idx = jnp.max(jnp.where(s == m, neg_iota, -jnp.inf), axis=0, keepdims=True)
        s = jnp.where(neg_iota == neg_idx, -jnp.inf, s)
        vals.append(m)
        idxs.append(-neg_idx)
    return vals, idxs


def _route_kernel(s1_ref, s2_ref, ia_ref, ib_ref, g_ref, ta_ref, tb_ref, tg_ref):
    R = s1_ref.shape[2]
    K = PEER_TOPK
    pad = _NCAND - len(_PAIRS)

    def head_body(h, carry):
        v1, i1 = _extract_top(s1_ref[h], K)
        v2, i2 = _extract_top(s2_ref[h], K)
        cand = jnp.concatenate([v1[a] + v2[b] for a, b in _PAIRS] + [jnp.full((pad, R), -jnp.inf, F32)], axis=0)
        ca = jnp.concatenate([i1[a] for a, b in _PAIRS] + [jnp.zeros((pad, R), F32)], axis=0)
        cb = jnp.concatenate([i2[b] for a, b in _PAIRS] + [jnp.zeros((pad, R), F32)], axis=0)
        best, pos = _extract_top(cand, K)
        iota = lax.broadcasted_iota(I32, cand.shape, 0).astype(F32)
        e = [jnp.exp(bv - best[0]) for bv in best]
        denom = e[0]
        for j in range(1, K):
            denom = denom + e[j]
        ia_rows, ib_rows, g_rows = [], [], []
        for j in range(K):
            hit = iota == pos[j]
            ia_rows.append(jnp.sum(jnp.where(hit, ca, 0.0), axis=0, keepdims=True))
            ib_rows.append(jnp.sum(jnp.where(hit, cb, 0.0), axis=0, keepdims=True))
            g_rows.append(e[j] / denom)
        rows = pl.ds(pl.multiple_of(h * K, K), K)
        ta_ref[rows, :] = jnp.concatenate(ia_rows, axis=0)
        tb_ref[rows, :] = jnp.concatenate(ib_rows, axis=0)
        tg_ref[rows, :] = jnp.concatenate(g_rows, axis=0)
        return carry

    lax.fori_loop(0, PEER_HEADS, head_body, 0)
    ia_ref[...] = ta_ref[...].T
    ib_ref[...] = tb_ref[...].T
    g_ref[...] = tg_ref[...].T


def _route(s1, s2):
    M = s1.shape[2]
    R = 512
    NS = PEER_HEADS * PEER_TOPK
    spec_in = pl.BlockSpec((PEER_HEADS, PEER_NKEYS, R), lambda i: (0, 0, i))
    spec_out = pl.BlockSpec((R, NS), lambda i: (i, 0))
    return pl.pallas_call(
        _route_kernel,
        grid=(M // R,),
        in_specs=[spec_in, spec_in],
        out_specs=[spec_out, spec_out, spec_out],
        out_shape=[jax.ShapeDtypeStruct((M, NS), F32)] * 3,
        scratch_shapes=[pltpu.VMEM((NS, R), F32)] * 3,
        compiler_params=_params(("parallel",)),
        name="route",
    )(s1, s2)


GM_PITCH = PEER_NKEYS // 2 + 4

def _peer_kernel(h2_ref, ia_ref, ib_ref, g_ref, u_ref, v_ref, x1_ref, g2_ref, fg_ref, o_ref,
                 gm_ref, acc_ref, *, T, AE, final_norm):
    e = pl.program_id(1)
    NK = PEER_NKEYS
    NA = NK // 2
    steps_per_build = NA // AE
    TOK_UNROLL = 32

    @pl.when(e == 0)
    def _():
        acc_ref[...] = jnp.zeros_like(acc_ref)

    @pl.when(e % steps_per_build == 0)
    def _():
        a_base = (e // steps_per_build) * NA
        sub_a = (lax.broadcasted_iota(I32, (NA, NK), 0) + a_base).astype(F32)
        sub_b = lax.broadcasted_iota(I32, (NK, NK), 0).astype(F32)

        zero = jnp.zeros((NA, NK), BF16)

        def tok_body(t8, carry):
            r0 = pl.multiple_of(t8 * TOK_UNROLL, TOK_UNROLL)
            base = pl.multiple_of(r0 * GM_PITCH, TOK_UNROLL)
            ia8 = ia_ref[pl.ds(r0, TOK_UNROLL), :]
            ib8 = ib_ref[pl.ds(r0, TOK_UNROLL), :]
            g8 = g_ref[pl.ds(r0, TOK_UNROLL), :]
            for j in range(0, TOK_UNROLL, 2):
                am = [jnp.where(sub_a == ia8[k:k + 1, :], g8[k:k + 1, :], 0.0).astype(BF16) for k in (j, j + 1)]
                bm = [jnp.where(sub_b == ib8[k:k + 1, :], 1.0, 0.0).astype(BF16) for k in (j, j + 1)]
                lhs = jnp.concatenate([jnp.concatenate([am[0], zero], axis=1),
                                       jnp.concatenate([zero, am[1]], axis=1)], axis=0)
                pair = _dot_nt(lhs, jnp.concatenate(bm, axis=1))
                gm_ref[pl.ds(base + j * GM_PITCH, NA), :] = pair[:NA]
                gm_ref[pl.ds(base + (j + 1) * GM_PITCH, NA), :] = pair[NA:]
            return carry

        lax.fori_loop(0, T // TOK_UNROLL, tok_body, 0)

    a_loc = (e % steps_per_build) * AE
    s = _dot_nt(h2_ref[...], u_ref[...])
    gt = jnp.concatenate([gm_ref[pl.ds(a_loc + j, T, stride=GM_PITCH), :] for j in range(AE)], axis=1)
    act = 0.5 * s * (1.0 + lax.erf(s * (0.5 ** 0.5)))
    acc_ref[...] += _dot((act * gt).astype(BF16), v_ref[...])

    @pl.when(e == pl.num_programs(1) - 1)
    def _():
        x2 = x1_ref[...] + g2_ref[0] * acc_ref[...]
        if final_norm:
            ms = jnp.mean(x2 * x2, axis=-1, keepdims=True)
            x2 = x2 * lax.rsqrt(ms + NORM_EPS) * fg_ref[...]
        o_ref[...] = x2


def _peer(h2, ia, ib, g, u, v, x1, ada3, final_g, S, final_norm):
    M, D = x1.shape
    T = min(512, S)
    AE = 8
    ET = AE * PEER_NKEYS
    NE = u.shape[0]
    NS = PEER_HEADS * PEER_TOPK
    tpb = S // T
    return pl.pallas_call(
        functools.partial(_peer_kernel, T=T, AE=AE, final_norm=final_norm),
        grid=(M // T, NE // ET),
        in_specs=[pl.BlockSpec((T, D), lambda i, e: (i, 0)),
                  pl.BlockSpec((T, NS), lambda i, e: (i, 0)),
                  pl.BlockSpec((T, NS), lambda i, e: (i, 0)),
                  pl.BlockSpec((T, NS), lambda i, e: (i, 0)),
                  pl.BlockSpec((ET, D), lambda i, e: (e, 0)),
                  pl.BlockSpec((ET, D), lambda i, e: (e, 0)),
                  pl.BlockSpec((T, D), lambda i, e: (i, 0)),
                  pl.BlockSpec((1, 1, D), lambda i, e: (i // tpb, 0, 5)),
                  pl.BlockSpec((1, D), lambda i, e: (0, 0))],
        out_specs=pl.BlockSpec((T, D), lambda i, e: (i, 0)),
        out_shape=jax.ShapeDtypeStruct((M, D), F32),
        scratch_shapes=[pltpu.VMEM((T * GM_PITCH, PEER_NKEYS), F32),
                        pltpu.VMEM((T, D), F32)],
        compiler_params=_params(("parallel", "arbitrary")),
        name="peer",
    )(h2, ia, ib, g, u, v, x1, ada3, final_g.reshape(1, D))


def _cat_w_in(w_in):
    sizes = (GLA_DK, GLA_DK, GLA_DV, GLA_DV, GLA_GATE_RANK, DSA_HEADS * DSA_HEAD_DIM, DSA_LATENT,
             IDX_HEADS * IDX_DIM, IDX_DIM, IDX_HEADS, D_MODEL, D_MODEL)
    offs = [0]
    for s in sizes:
        offs.append(offs[-1] + s)
    gq, gk, gv, gr, glow, dq, dkv, iq, ik, iw, za, zb = [w_in[:, offs[i]:offs[i + 1]] for i in range(len(sizes))]
    pad_misc = jnp.zeros((w_in.shape[0], 128 - (GLA_GATE_RANK + IDX_DIM + IDX_HEADS)), w_in.dtype)
    pad_end = jnp.zeros((w_in.shape[0], PROJ_COLS - (COL_MISC + 128)), w_in.dtype)
    return jnp.concatenate([gv, gr, dq, za, zb, gq, gk, iq, dkv, glow, ik, iw, pad_misc, pad_end], axis=1).astype(BF16)


def kernel(x, c, w_ada, b_ada, norm1_g, w_in, gla_w_gate_up, gla_b_gate, gla_norm_g, dsa_kv_norm_g, dsa_w_uk,
           dsa_w_uv, w_branch_a, w_branch_b, w_out, norm2_g, peer_w_q, peer_sub_keys_1, peer_sub_keys_2,
           peer_u, peer_v, final_norm_g):
    B, S, D = x.shape
    depth = w_ada.shape[0]
    x2 = x.reshape(B * S, D)
    for l in range(depth):
        ada3 = _ada(c, w_ada[l], b_ada[l]).reshape(B, 1, 6 * D)
        proj = _proj(x2, ada3, norm1_g[l], _cat_w_in(w_in[l]), S)
        ya = _gla(proj, gla_w_gate_up[l], gla_b_gate[l], gla_norm_g[l], B, S)
        ckv, ckvt, ikt = _kvprep(proj, dsa_kv_norm_g[l], _dsa_key_chunk(S))
        yb = _dsa(proj, ikt, ckv, ckvt, dsa_w_uk[l].astype(BF16), dsa_w_uv[l].astype(BF16), B, S)
        x1, h2, s1, s2 = _merge(ya, yb, proj, x2, ada3, norm2_g[l],
                                w_branch_a[l].astype(BF16), w_branch_b[l].astype(BF16), w_out[l].astype(BF16),
                                peer_w_q[l].astype(BF16), peer_sub_keys_1[l].astype(BF16),
                                peer_sub_keys_2[l].astype(BF16), S)
        ia, ib, g = _route(s1, s2)
        x2 = _peer(h2, ia, ib, g, peer_u[l].astype(BF16), peer_v[l].astype(BF16), x1, ada3, final_norm_g, S,
                   final_norm=(l == depth - 1))
    return x2.reshape(B, S, D)
```

```python
import functools

import jax
import jax.numpy as jnp
from jax import lax
from jax.experimental import pallas as pl
from jax.experimental.pallas import tpu as pltpu

F32 = jnp.float32
BF16 = jnp.bfloat16
I32 = jnp.int32

D_MODEL = 1024
GLA_HEADS = 4
GLA_DK = D_MODEL // 2
GLA_DV = D_MODEL
GLA_GATE_RANK = 16
GLA_TAU = 16.0
GLA_CHUNK = 64
DSA_HEADS = 8
DSA_HEAD_DIM = 128
DSA_LATENT = 256
IDX_HEADS = 8
IDX_DIM = 64
DSA_TOPK = 256
Q_BLOCK = 128
PEER_HEADS = 8
PEER_NKEYS = 128
PEER_DKEY = 256
PEER_TOPK = 16
NORM_EPS = 1e-6

INT_MIN = -(2 ** 31)
MASK_NEG = -1e30
MASK_DIST = 1e30

COL_GV, COL_GR, COL_DQ, COL_ZA, COL_ZB = 0, 1024, 2048, 3072, 4096
COL_GQ, COL_GK, COL_IQ, COL_DKV, COL_MISC = 5120, 5632, 6144, 6656, 6912
PROJ_COLS = 7168
MISC_GLOW, MISC_IK, MISC_IW = 0, 16, 80

VMEM_LIMIT = 56 * 1024 * 1024


def _params(sem):
    return pltpu.CompilerParams(dimension_semantics=sem, vmem_limit_bytes=VMEM_LIMIT)


def _dot(a, b):
    return jnp.dot(a, b, preferred_element_type=F32)


def _dot_nt(a, b):
    return lax.dot_general(a, b, (((1,), (1,)), ((), ())), preferred_element_type=F32)


def _dot_tn(a, b):
    return lax.dot_general(a, b, (((0,), (0,)), ((), ())), preferred_element_type=F32)


def _ada_kernel(c_ref, w_ref, b_ref, o_ref):
    c = c_ref[...]
    a = c * jax.nn.sigmoid(c)
    o_ref[...] = _dot(a.astype(BF16), w_ref[...].astype(BF16)) + b_ref[...]


def _ada(c, w_ada, b_ada):
    B, D = c.shape
    N = w_ada.shape[1]
    tn = 1024
    return pl.pallas_call(
        _ada_kernel,
        grid=(N // tn,),
        in_specs=[pl.BlockSpec((B, D), lambda j: (0, 0)),
                  pl.BlockSpec((D, tn), lambda j: (0, j)),
                  pl.BlockSpec((1, tn), lambda j: (0, j))],
        out_specs=pl.BlockSpec((B, tn), lambda j: (0, j)),
        out_shape=jax.ShapeDtypeStruct((B, N), F32),
        compiler_params=_params(("parallel",)),
        name="ada",
    )(c, w_ada, b_ada.reshape(1, N))


def _proj_kernel(x_ref, sc_ref, sh_ref, g_ref, w_ref, o_ref, h_ref):
    @pl.when(pl.program_id(1) == 0)
    def _():
        x = x_ref[...]
        ms = jnp.mean(x * x, axis=-1, keepdims=True)
        y = x * lax.rsqrt(ms + NORM_EPS) * g_ref[...]
        h_ref[...] = (y * (1.0 + sc_ref[0]) + sh_ref[0]).astype(BF16)

    o_ref[...] = _dot(h_ref[...], w_ref[...]).astype(BF16)


def _proj(x2, ada3, norm_g, w_cat, S):
    M, D = x2.shape
    N = w_cat.shape[1]
    tm, tn = min(2048, S), 512
    tpb = S // tm
    return pl.pallas_call(
        _proj_kernel,
        grid=(M // tm, N // tn),
        in_specs=[pl.BlockSpec((tm, D), lambda i, j: (i, 0)),
                  pl.BlockSpec((1, 1, D), lambda i, j: (i // tpb, 0, 1)),
                  pl.BlockSpec((1, 1, D), lambda i, j: (i // tpb, 0, 0)),
                  pl.BlockSpec((1, D), lambda i, j: (0, 0)),
                  pl.BlockSpec((D, tn), lambda i, j: (0, j))],
        out_specs=pl.BlockSpec((tm, tn), lambda i, j: (i, j)),
        out_shape=jax.ShapeDtypeStruct((M, N), BF16),
        scratch_shapes=[pltpu.VMEM((tm, D), BF16)],
        compiler_params=_params(("parallel", "arbitrary")),
        name="proj",
    )(x2, ada3, ada3, norm_g.reshape(1, D), w_cat)


def _gla_kernel(q_ref, k_ref, v_ref, r_ref, misc_ref, wg_ref, bg_ref, ng_ref, o_ref, st_ref, *, n_chunks):
    C = GLA_CHUNK
    H = GLA_HEADS
    dkh = GLA_DK // H
    dvh = GLA_DV // H

    @pl.when(pl.program_id(1) == 0)
    def _():
        st_ref[...] = jnp.zeros_like(st_ref)

    wg = wg_ref[...].astype(BF16)
    bg = bg_ref[...]
    ng = ng_ref[...]
    row = lax.broadcasted_iota(I32, (C, C), 0)
    col = lax.broadcasted_iota(I32, (C, C), 1)
    causal = row >= col
    tril = jnp.where(causal, 1.0, 0.0).astype(BF16)

    st = [st_ref[h] for h in range(H)]
    for c in range(n_chunks):
        sl = slice(c * C, (c + 1) * C)
        glow = misc_ref[sl, MISC_GLOW:MISC_GLOW + GLA_GATE_RANK]
        gate = _dot(glow, wg) + bg
        la = (jnp.minimum(gate, 0.0) - jnp.log1p(jnp.exp(-jnp.abs(gate)))) * (1.0 / GLA_TAU)
        la_hi = la.astype(BF16)
        la_lo = (la - la_hi.astype(F32)).astype(BF16)
        b = _dot(tril, la_hi) + _dot(tril, la_lo)
        b_last = b[C - 1:C, :]
        q_dec = (q_ref[sl, :].astype(F32) * (dkh ** -0.5) * jnp.exp(b)).astype(BF16)
        k = k_ref[sl, :].astype(F32)
        k_inv = (k * jnp.exp(-b)).astype(BF16)
        k_end = (k * jnp.exp(b_last - b)).astype(BF16)
        decay = jnp.exp(b_last)
        v = v_ref[sl, :]
        r = r_ref[sl, :].astype(F32)
        gate_r = r * jax.nn.sigmoid(r)
        for h in range(H):
            ks = slice(h * dkh, (h + 1) * dkh)
            vs = slice(h * dvh, (h + 1) * dvh)
            attn = jnp.where(causal, _dot_nt(q_dec[:, ks], k_inv[:, ks]), 0.0)
            o = _dot(attn.astype(BF16), v[:, vs]) + _dot_nt(q_dec[:, ks], st[h].astype(BF16))
            st[h] = decay[:, ks] * st[h] + _dot_tn(v[:, vs], k_end[:, ks])
            ms = jnp.mean(o * o, axis=-1, keepdims=True)
            on = o * lax.rsqrt(ms + NORM_EPS) * ng[:, vs]
            o_ref[sl, vs] = (on * gate_r[:, vs]).astype(BF16)
    for h in range(H):
        st_ref[h] = st[h]


def _gla(proj, w_gate_up, b_gate, norm_g, B, S):
    M = proj.shape[0]
    rb = min(256, S)
    nrb = S // rb
    dkh = GLA_DK // GLA_HEADS
    dvh = GLA_DV // GLA_HEADS
    rowmap = lambda b, c: b * nrb + c
    return pl.pallas_call(
        functools.partial(_gla_kernel, n_chunks=rb // GLA_CHUNK),
        grid=(B, nrb),
        in_specs=[pl.BlockSpec((rb, GLA_DK), lambda b, c: (rowmap(b, c), COL_GQ // GLA_DK)),
                  pl.BlockSpec((rb, GLA_DK), lambda b, c: (rowmap(b, c), COL_GK // GLA_DK)),
                  pl.BlockSpec((rb, GLA_DV), lambda b, c: (rowmap(b, c), COL_GV // GLA_DV)),
                  pl.BlockSpec((rb, GLA_DV), lambda b, c: (rowmap(b, c), COL_GR // GLA_DV)),
                  pl.BlockSpec((rb, 128), lambda b, c: (rowmap(b, c), COL_MISC // 128)),
                  pl.BlockSpec((GLA_GATE_RANK, GLA_DK), lambda b, c: (0, 0)),
                  pl.BlockSpec((1, GLA_DK), lambda b, c: (0, 0)),
                  pl.BlockSpec((1, GLA_DV), lambda b, c: (0, 0))],
        out_specs=pl.BlockSpec((rb, GLA_DV), lambda b, c: (rowmap(b, c), 0)),
        out_shape=jax.ShapeDtypeStruct((M, GLA_DV), BF16),
        scratch_shapes=[pltpu.VMEM((GLA_HEADS, dvh, dkh), F32)],
        compiler_params=_params(("parallel", "arbitrary")),
        name="gla",
    )(proj, proj, proj, proj, proj, w_gate_up, b_gate.reshape(1, GLA_DK), norm_g.reshape(1, GLA_DV))


def _kvprep_kernel(kv_ref, misc_ref, g_ref, ckv_ref, ckvt_ref, ikt_ref):
    x = kv_ref[...].astype(F32)
    ms = jnp.mean(x * x, axis=-1, keepdims=True)
    c = x * lax.rsqrt(ms + NORM_EPS) * g_ref[...]
    ckv_ref[...] = c.astype(BF16)
    ckvt_ref[0] = c.T.astype(BF16)
    ikt_ref[0] = misc_ref[:, MISC_IK:MISC_IK + IDX_DIM].astype(F32).T.astype(BF16)


def _kvprep(proj, kv_norm_g, KC):
    M = proj.shape[0]
    return pl.pallas_call(
        _kvprep_kernel,
        grid=(M // KC,),
        in_specs=[pl.BlockSpec((KC, DSA_LATENT), lambda i: (i, COL_DKV // DSA_LATENT)),
                  pl.BlockSpec((KC, 128), lambda i: (i, COL_MISC // 128)),
                  pl.BlockSpec((1, DSA_LATENT), lambda i: (0, 0))],
        out_specs=[pl.BlockSpec((KC, DSA_LATENT), lambda i: (i, 0)),
                   pl.BlockSpec((1, DSA_LATENT, KC), lambda i: (i, 0, 0)),
                   pl.BlockSpec((1, IDX_DIM, KC), lambda i: (i, 0, 0))],
        out_shape=[jax.ShapeDtypeStruct((M, DSA_LATENT), BF16),
                   jax.ShapeDtypeStruct((M // KC, DSA_LATENT, KC), BF16),
                   jax.ShapeDtypeStruct((M // KC, IDX_DIM, KC), BF16)],
        compiler_params=_params(("parallel",)),
        name="kvprep",
    )(proj, proj, kv_norm_g.reshape(1, DSA_LATENT))


def _dsa_kernel(dq_ref, iq_ref, misc_ref, ikt_ref, ckv_ref, ckvt_ref, wuk_ref, wuv_ref, o_ref,
                keys_ref, qlat_ref, m_ref, l_ref, acc_ref, thr_ref, tie_ref, *, S, KC, k_sel):
    QB = Q_BLOCK
    H = DSA_HEADS
    qb = pl.program_id(1)
    nkc = (qb * QB + QB + KC - 1) // KC
    t_pos = qb * QB + lax.broadcasted_iota(I32, (QB, 1), 0)
    lane_pos = lax.broadcasted_iota(I32, (1, KC), 1)

    iw = misc_ref[:, MISC_IW:MISC_IW + IDX_HEADS].astype(F32) * (IDX_HEADS ** -0.5) * (IDX_DIM ** -0.5)
    iq = iq_ref[...]

    def score_body(kc, carry):
        k0 = pl.multiple_of(kc * KC, KC)
        ikc = ikt_ref[kc]
        score = jnp.zeros((QB, KC), F32)
        for h in range(IDX_HEADS):
            rel = _dot(iq[:, h * IDX_DIM:(h + 1) * IDX_DIM], ikc)
            rel = jnp.maximum(rel, 0.0)
            score = score + iw[:, h:h + 1] * rel
        valid = (k0 + lane_pos) <= t_pos
        keys_ref[kc] = jnp.where(valid, score, -jnp.inf)
        return carry

    lax.fori_loop(0, nkc, score_body, 0)

    def count(pred):
        def body(kc, acc):
            kv = keys_ref[kc]
            for g in range(KC // 128):
                acc = acc + jnp.where(pred(kv[:, g * 128:(g + 1) * 128], kc * KC + g * 128), 1.0, 0.0)
            return acc
        acc = lax.fori_loop(0, nkc, body, jnp.zeros((QB, 128), F32))
        return jnp.sum(acc, axis=1, keepdims=True)

    def as_float(code):
        return pltpu.bitcast(code ^ ((code >> 31) & 0x7FFFFFFF), F32)

    def bit_body(i, code):
        cand = code + (jnp.int32(1) << (31 - i))
        cand_f = as_float(cand)
        cnt = count(lambda kv, base: kv >= cand_f)
        return jnp.where(cnt >= k_sel, cand, code)

    code = lax.fori_loop(0, 32, bit_body, jnp.full((QB, 1), INT_MIN, I32))
    thr = jnp.where(code == INT_MIN, -jnp.inf, as_float(code))
    thr_ref[...] = thr
    n_gt = count(lambda kv, base: kv > thr)
    n_ge = count(lambda kv, base: kv >= thr)
    tie_ref[...] = jnp.full((QB, 1), S, I32)

    @pl.when(jnp.max(n_ge) > k_sel)
    def _():
        need = k_sel - n_gt
        lane = lax.broadcasted_iota(I32, (1, 128), 1)

        def pos_body(i, p):
            cand = p + (jnp.int32(1) << (S.bit_length() - 1 - i))
            cnt = count(lambda kv, base: (kv == thr) & ((base + lane) < cand))
            return jnp.where(cnt < need, cand, p)

        tie_ref[...] = lax.fori_loop(0, S.bit_length(), pos_body, jnp.zeros((QB, 1), I32))

    thr = thr_ref[...]
    tie = tie_ref[...]

    for h in range(H):
        ql = _dot(dq_ref[:, h * DSA_HEAD_DIM:(h + 1) * DSA_HEAD_DIM], wuk_ref[h])
        qlat_ref[h * QB:(h + 1) * QB, :] = ql.astype(BF16)
    m_ref[...] = jnp.full(m_ref.shape, MASK_NEG, F32)
    l_ref[...] = jnp.zeros(l_ref.shape, F32)
    acc_ref[...] = jnp.zeros(acc_ref.shape, F32)

    log2e = 1.4426950408889634
    c_raw = (DSA_HEAD_DIM ** -0.5) * log2e
    HG = 2
    head = lax.broadcasted_iota(I32, (H * QB, 1), 0) // QB
    slope2 = jnp.exp2((head + 1).astype(F32) * (-8.0 / H)) * log2e

    def att_body(kc, carry):
        k0 = pl.multiple_of(kc * KC, KC)
        kv = ckv_ref[pl.ds(k0, KC), :]
        keyv = keys_ref[kc]
        kpos = k0 + lane_pos
        dist = t_pos - kpos
        sel = ((keyv > thr) | ((keyv == thr) & (kpos <= tie))) & (dist >= 0)
        dmask = jnp.where(sel, dist.astype(F32), MASK_DIST)
        dmask = jnp.concatenate([dmask] * HG, axis=0)
        kvt = ckvt_ref[kc]
        for g in range(H // HG):
            rows = slice(g * HG * QB, (g + 1) * HG * QB)
            lg = _dot(qlat_ref[rows], kvt) * c_raw - slope2[rows] * dmask
            m_old = m_ref[rows]
            m_new = jnp.maximum(m_old, jnp.max(lg, axis=1, keepdims=True))
            alpha = jnp.exp2(m_old - m_new)
            p = jnp.exp2(lg - m_new)
            l_ref[rows] = alpha * l_ref[rows] + jnp.sum(p, axis=1, keepdims=True)
            acc_ref[rows] = alpha * acc_ref[rows] + _dot(p.astype(BF16), kv)
            m_ref[rows] = m_new
        return carry

    lax.fori_loop(0, nkc, att_body, 0)

    for h in range(H):
        rows = slice(h * QB, (h + 1) * QB)
        o_lat = acc_ref[rows] / l_ref[rows]
        o_ref[:, h * DSA_HEAD_DIM:(h + 1) * DSA_HEAD_DIM] = _dot(o_lat.astype(BF16), wuv_ref[h]).astype(BF16)


def _dsa_key_chunk(S):
    return min(1024, S)


def _dsa(proj, ikt, ckv, ckvt, w_uk, w_uv, B, S):
    M = proj.shape[0]
    QB = Q_BLOCK
    nqb = S // QB
    KC = _dsa_key_chunk(S)
    k_sel = min(DSA_TOPK, S // 4)
    HD = DSA_HEADS * DSA_HEAD_DIM
    return pl.pallas_call(
        functools.partial(_dsa_kernel, S=S, KC=KC, k_sel=k_sel),
        grid=(B, nqb),
        in_specs=[pl.BlockSpec((QB, HD), lambda b, q: (b * nqb + q, COL_DQ // HD)),
                  pl.BlockSpec((QB, IDX_HEADS * IDX_DIM), lambda b, q: (b * nqb + q, COL_IQ // (IDX_HEADS * IDX_DIM))),
                  pl.BlockSpec((QB, 128), lambda b, q: (b * nqb + q, COL_MISC // 128)),
                  pl.BlockSpec((S // KC, IDX_DIM, KC), lambda b, q: (b, 0, 0)),
                  pl.BlockSpec((S, DSA_LATENT), lambda b, q: (b, 0)),
                  pl.BlockSpec((S // KC, DSA_LATENT, KC), lambda b, q: (b, 0, 0)),
                  pl.BlockSpec((DSA_HEADS, DSA_HEAD_DIM, DSA_LATENT), lambda b, q: (0, 0, 0)),
                  pl.BlockSpec((DSA_HEADS, DSA_LATENT, DSA_HEAD_DIM), lambda b, q: (0, 0, 0))],
        out_specs=pl.BlockSpec((QB, HD), lambda b, q: (b * nqb + q, 0)),
        out_shape=jax.ShapeDtypeStruct((M, HD), BF16),
        scratch_shapes=[pltpu.VMEM((S // KC, QB, KC), F32),
                        pltpu.VMEM((DSA_HEADS * QB, DSA_LATENT), BF16),
                        pltpu.VMEM((DSA_HEADS * QB, 1), F32),
                        pltpu.VMEM((DSA_HEADS * QB, 1), F32),
                        pltpu.VMEM((DSA_HEADS * QB, DSA_LATENT), F32),
                        pltpu.VMEM((QB, 1), F32),
                        pltpu.VMEM((QB, 1), I32)],
        compiler_params=_params(("parallel", "arbitrary")),
        name="dsa",
    )(proj, proj, proj, ikt, ckv, ckvt, w_uk, w_uv)


def _merge_kernel(ya_ref, yb_ref, za_ref, zb_ref, x_ref, g1_ref, sc2_ref, sh2_ref, n2_ref,
                  wa_ref, wb_ref, wo_ref, wq_ref, k1_ref, k2_ref,
                  x1_ref, h2_ref, s1_ref, s2_ref):
    ya = _dot(ya_ref[...], wa_ref[...])
    yb = _dot(yb_ref[...], wb_ref[...])
    mix = jax.nn.sigmoid(za_ref[...].astype(F32)) * ya + jax.nn.sigmoid(zb_ref[...].astype(F32)) * yb
    y = _dot(mix.astype(BF16), wo_ref[...])
    x1 = x_ref[...] + g1_ref[0] * y
    x1_ref[...] = x1
    ms = jnp.mean(x1 * x1, axis=-1, keepdims=True)
    h2 = (x1 * lax.rsqrt(ms + NORM_EPS) * n2_ref[...]) * (1.0 + sc2_ref[0]) + sh2_ref[0]
    h2b = h2.astype(BF16)
    h2_ref[...] = h2b
    q = _dot(h2b, wq_ref[...]).astype(BF16)
    half = PEER_DKEY // 2
    for h in range(PEER_HEADS):
        s1_ref[h] = _dot_nt(k1_ref[...], q[:, h * PEER_DKEY:h * PEER_DKEY + half])
        s2_ref[h] = _dot_nt(k2_ref[...], q[:, h * PEER_DKEY + half:(h + 1) * PEER_DKEY])


def _merge(ya, yb, proj, x2, ada3, norm2_g, wa, wb, wo, wq, k1, k2, S):
    M, D = x2.shape
    tm = min(512, S)
    tpb = S // tm
    full = lambda shape: pl.BlockSpec(shape, lambda i: (0,) * len(shape))
    adaspec = lambda k: pl.BlockSpec((1, 1, D), lambda i: (i // tpb, 0, k))
    return pl.pallas_call(
        _merge_kernel,
        grid=(M // tm,),
        in_specs=[pl.BlockSpec((tm, D), lambda i: (i, 0)),
                  pl.BlockSpec((tm, D), lambda i: (i, 0)),
                  pl.BlockSpec((tm, D), lambda i: (i, COL_ZA // D)),
                  pl.BlockSpec((tm, D), lambda i: (i, COL_ZB // D)),
                  pl.BlockSpec((tm, D), lambda i: (i, 0)),
                  adaspec(2), adaspec(4), adaspec(3),
                  full((1, D)),
                  full((D, D)), full((D, D)), full((D, D)), full((D, PEER_HEADS * PEER_DKEY)),
                  full((PEER_NKEYS, PEER_DKEY // 2)), full((PEER_NKEYS, PEER_DKEY // 2))],
        out_specs=[pl.BlockSpec((tm, D), lambda i: (i, 0)),
                   pl.BlockSpec((tm, D), lambda i: (i, 0)),
                   pl.BlockSpec((PEER_HEADS, PEER_NKEYS, tm), lambda i: (0, 0, i)),
                   pl.BlockSpec((PEER_HEADS, PEER_NKEYS, tm), lambda i: (0, 0, i))],
        out_shape=[jax.ShapeDtypeStruct((M, D), F32),
                   jax.ShapeDtypeStruct((M, D), BF16),
                   jax.ShapeDtypeStruct((PEER_HEADS, PEER_NKEYS, M), F32),
                   jax.ShapeDtypeStruct((PEER_HEADS, PEER_NKEYS, M), F32)],
        compiler_params=_params(("parallel",)),
        name="merge",
    )(ya, yb, proj, proj, x2, ada3, ada3, ada3, norm2_g.reshape(1, D), wa, wb, wo, wq, k1, k2)


_PAIRS = [(j1, j2) for j1 in range(PEER_TOPK) for j2 in range(PEER_TOPK) if (j1 + 1) * (j2 + 1) <= PEER_TOPK]
_NCAND = -(-len(_PAIRS) // 8) * 8


def _extract_top(s, rounds):
    neg_iota = (-lax.broadcasted_iota(I32, s.shape, 0)).astype(F32)
    vals, idxs = [], []
    for _ in range(rounds):
        m = jnp.max(s, axis=0, keepdims=True)
        neg_idx = jnp.max(jnp.where(s == m, neg_iota, -jnp.inf), axis=0, keepdims=True)
        s = jnp.where(neg_iota == neg_idx, -jnp.inf, s)
        vals.append(m)
        idxs.append(-neg_idx)
    return vals, idxs


def _route_kernel(s1_ref, s2_ref, ia_ref, ib_ref, g_ref, ta_ref, tb_ref, tg_ref):
    R = s1_ref.shape[2]
    K = PEER_TOPK
    pad = _NCAND - len(_PAIRS)

    def head_body(h, carry):
        v1, i1 = _extract_top(s1_ref[h], K)
        v2, i2 = _extract_top(s2_ref[h], K)
        cand = jnp.concatenate([v1[a] + v2[b] for a, b in _PAIRS] + [jnp.full((pad, R), -jnp.inf, F32)], axis=0)
        ca = jnp.concatenate([i1[a] for a, b in _PAIRS] + [jnp.zeros((pad, R), F32)], axis=0)
        cb = jnp.concatenate([i2[b] for a, b in _PAIRS] + [jnp.zeros((pad, R), F32)], axis=0)
        best, pos = _extract_top(cand, K)
        iota = lax.broadcasted_iota(I32, cand.shape, 0).astype(F32)
        e = [jnp.exp(bv - best[0]) for bv in best]
        denom = e[0]
        for j in range(1, K):
            denom = denom + e[j]
        ia_rows, ib_rows, g_rows = [], [], []
        for j in range(K):
            hit = iota == pos[j]
            ia_rows.append(jnp.sum(jnp.where(hit, ca, 0.0), axis=0, keepdims=True))
            ib_rows.append(jnp.sum(jnp.where(hit, cb, 0.0), axis=0, keepdims=True))
            g_rows.append(e[j] / denom)
        rows = pl.ds(pl.multiple_of(h * K, K), K)
        ta_ref[rows, :] = jnp.concatenate(ia_rows, axis=0)
        tb_ref[rows, :] = jnp.concatenate(ib_rows, axis=0)
        tg_ref[rows, :] = jnp.concatenate(g_rows, axis=0)
        return carry

    lax.fori_loop(0, PEER_HEADS, head_body, 0)
    ia_ref[...] = ta_ref[...].T
    ib_ref[...] = tb_ref[...].T
    g_ref[...] = tg_ref[...].T


def _route(s1, s2):
    M = s1.shape[2]
    R = 512
    NS = PEER_HEADS * PEER_TOPK
    spec_in = pl.BlockSpec((PEER_HEADS, PEER_NKEYS, R), lambda i: (0, 0, i))
    spec_out = pl.BlockSpec((R, NS), lambda i: (i, 0))
    return pl.pallas_call(
        _route_kernel,
        grid=(M // R,),
        in_specs=[spec_in, spec_in],
        out_specs=[spec_out, spec_out, spec_out],
        out_shape=[jax.ShapeDtypeStruct((M, NS), F32)] * 3,
        scratch_shapes=[pltpu.VMEM((NS, R), F32)] * 3,
        compiler_params=_params(("parallel",)),
        name="route",
    )(s1, s2)


GM_PITCH = PEER_NKEYS // 2 + 4

def _peer_kernel(h2_ref, ia_ref, ib_ref, g_ref, u_ref, v_ref, x1_ref, g2_ref, fg_ref, o_ref,
                 gm_ref, acc_ref, *, T, AE, final_norm):
    e = pl.program_id(1)
    NK = PEER_NKEYS
    NA = NK // 2
    steps_per_build = NA // AE
    TOK_UNROLL = 32

    @pl.when(e == 0)
    def _():
        acc_ref[...] = jnp.zeros_like(acc_ref)

    @pl.when(e % steps_per_build == 0)
    def _():
        a_base = (e // steps_per_build) * NA
        sub_a = (lax.broadcasted_iota(I32, (NA, NK), 0) + a_base).astype(F32)
        sub_b = lax.broadcasted_iota(I32, (NK, NK), 0).astype(F32)

        zero = jnp.zeros((NA, NK), BF16)

        def tok_body(t8, carry):
            r0 = pl.multiple_of(t8 * TOK_UNROLL, TOK_UNROLL)
            base = pl.multiple_of(r0 * GM_PITCH, TOK_UNROLL)
            ia8 = ia_ref[pl.ds(r0, TOK_UNROLL), :]
            ib8 = ib_ref[pl.ds(r0, TOK_UNROLL), :]
            g8 = g_ref[pl.ds(r0, TOK_UNROLL), :]
            for j in range(0, TOK_UNROLL, 2):
                am = [jnp.where(sub_a == ia8[k:k + 1, :], g8[k:k + 1, :], 0.0).astype(BF16) for k in (j, j + 1)]
                bm = [jnp.where(sub_b == ib8[k:k + 1, :], 1.0, 0.0).astype(BF16) for k in (j, j + 1)]
                lhs = jnp.concatenate([jnp.concatenate([am[0], zero], axis=1),
                                       jnp.concatenate([zero, am[1]], axis=1)], axis=0)
                pair = _dot_nt(lhs, jnp.concatenate(bm, axis=1))
                gm_ref[pl.ds(base + j * GM_PITCH, NA), :] = pair[:NA]
                gm_ref[pl.ds(base + (j + 1) * GM_PITCH, NA), :] = pair[NA:]
            return carry

        lax.fori_loop(0, T // TOK_UNROLL, tok_body, 0)

    a_loc = (e % steps_per_build) * AE
    s = _dot_nt(h2_ref[...], u_ref[...])
    gt = jnp.concatenate([gm_ref[pl.ds(a_loc + j, T, stride=GM_PITCH), :] for j in range(AE)], axis=1)
    act = 0.5 * s * (1.0 + lax.erf(s * (0.5 ** 0.5)))
    acc_ref[...] += _dot((act * gt).astype(BF16), v_ref[...])

    @pl.when(e == pl.num_programs(1) - 1)
    def _():
        x2 = x1_ref[...] + g2_ref[0] * acc_ref[...]
        if final_norm:
            ms = jnp.mean(x2 * x2, axis=-1, keepdims=True)
            x2 = x2 * lax.rsqrt(ms + NORM_EPS) * fg_ref[...]
        o_ref[...] = x2


def _peer(h2, ia, ib, g, u, v, x1, ada3, final_g, S, final_norm):
    M, D = x1.shape
    T = min(512, S)
    AE = 8
    ET = AE * PEER_NKEYS
    NE = u.shape[0]
    NS = PEER_HEADS * PEER_TOPK
    tpb = S // T
    return pl.pallas_call(
        functools.partial(_peer_kernel, T=T, AE=AE, final_norm=final_norm),
        grid=(M // T, NE // ET),
        in_specs=[pl.BlockSpec((T, D), lambda i, e: (i, 0)),
                  pl.BlockSpec((T, NS), lambda i, e: (i, 0)),
                  pl.BlockSpec((T, NS), lambda i, e: (i, 0)),
                  pl.BlockSpec((T, NS), lambda i, e: (i, 0)),
                  pl.BlockSpec((ET, D), lambda i, e: (e, 0)),
                  pl.BlockSpec((ET, D), lambda i, e: (e, 0)),
                  pl.BlockSpec((T, D), lambda i, e: (i, 0)),
                  pl.BlockSpec((1, 1, D), lambda i, e: (i // tpb, 0, 5)),
                  pl.BlockSpec((1, D), lambda i, e: (0, 0))],
        out_specs=pl.BlockSpec((T, D), lambda i, e: (i, 0)),
        out_shape=jax.ShapeDtypeStruct((M, D), F32),
        scratch_shapes=[pltpu.VMEM((T * GM_PITCH, PEER_NKEYS), F32),
                        pltpu.VMEM((T, D), F32)],
        compiler_params=_params(("parallel", "arbitrary")),
        name="peer",
    )(h2, ia, ib, g, u, v, x1, ada3, final_g.reshape(1, D))


def _cat_w_in(w_in):
    sizes = (GLA_DK, GLA_DK, GLA_DV, GLA_DV, GLA_GATE_RANK, DSA_HEADS * DSA_HEAD_DIM, DSA_LATENT,
             IDX_HEADS * IDX_DIM, IDX_DIM, IDX_HEADS, D_MODEL, D_MODEL)
    offs = [0]
    for s in sizes:
        offs.append(offs[-1] + s)
    gq, gk, gv, gr, glow, dq, dkv, iq, ik, iw, za, zb = [w_in[:, offs[i]:offs[i + 1]] for i in range(len(sizes))]
    pad_misc = jnp.zeros((w_in.shape[0], 128 - (GLA_GATE_RANK + IDX_DIM + IDX_HEADS)), w_in.dtype)
    pad_end = jnp.zeros((w_in.shape[0], PROJ_COLS - (COL_MISC + 128)), w_in.dtype)
    return jnp.concatenate([gv, gr, dq, za, zb, gq, gk, iq, dkv, glow, ik, iw, pad_misc, pad_end], axis=1).astype(BF16)


def kernel(x, c, w_ada, b_ada, norm1_g, w_in, gla_w_gate_up, gla_b_gate, gla_norm_g, dsa_kv_norm_g, dsa_w_uk,
           dsa_w_uv, w_branch_a, w_branch_b, w_out, norm2_g, peer_w_q, peer_sub_keys_1, peer_sub_keys_2,
           peer_u, peer_v, final_norm_g):
    B, S, D = x.shape
    depth = w_ada.shape[0]
    x2 = x.reshape(B * S, D)
    for l in range(depth):
        ada3 = _ada(c, w_ada[l], b_ada[l]).reshape(B, 1, 6 * D)
        proj = _proj(x2, ada3, norm1_g[l], _cat_w_in(w_in[l]), S)
        ya = _gla(proj, gla_w_gate_up[l], gla_b_gate[l], gla_norm_g[l], B, S)
        ckv, ckvt, ikt = _kvprep(proj, dsa_kv_norm_g[l], _dsa_key_chunk(S))
        yb = _dsa(proj, ikt, ckv, ckvt, dsa_w_uk[l].astype(BF16), dsa_w_uv[l].astype(BF16), B, S)
        x1, h2, s1, s2 = _merge(ya, yb, proj, x2, ada3, norm2_g[l],
                                w_branch_a[l].astype(BF16), w_branch_b[l].astype(BF16), w_out[l].astype(BF16),
                                peer_w_q[l].astype(BF16), peer_sub_keys_1[l].astype(BF16),
                                peer_sub_keys_2[l].astype(BF16), S)
        ia, ib, g = _route(s1, s2)
        x2 = _peer(h2, ia, ib, g, peer_u[l].astype(BF16), peer_v[l].astype(BF16), x1, ada3, final_norm_g, S,
                   final_norm=(l == depth - 1))
    return x2.reshape(B, S, D)
```

```python
import functools

import jax
import jax.numpy as jnp
from jax import lax
from jax.experimental import pallas as pl
from jax.experimental.pallas import tpu as pltpu

F32 = jnp.float32
BF16 = jnp.bfloat16
I32 = jnp.int32

D_MODEL = 1024
GLA_HEADS = 4
GLA_DK = D_MODEL // 2
GLA_DV = D_MODEL
GLA_GATE_RANK = 16
GLA_TAU = 16.0
GLA_CHUNK = 64
DSA_HEADS = 8
DSA_HEAD_DIM = 128
DSA_LATENT = 256
IDX_HEADS = 8
IDX_DIM = 64
DSA_TOPK = 256
Q_BLOCK = 128
PEER_HEADS = 8
PEER_NKEYS = 128
PEER_DKEY = 256
PEER_TOPK = 16
NORM_EPS = 1e-6

INT_MIN = -(2 ** 31)
MASK_NEG = -1e30
MASK_DIST = 1e30

COL_GV, COL_GR, COL_DQ, COL_ZA, COL_ZB = 0, 1024, 2048, 3072, 4096
COL_GQ, COL_GK, COL_IQ, COL_DKV, COL_MISC = 5120, 5632, 6144, 6656, 6912
PROJ_COLS = 7168
MISC_GLOW, MISC_IK, MISC_IW = 0, 16, 80

VMEM_LIMIT = 56 * 1024 * 1024


def _params(sem):
    return pltpu.CompilerParams(dimension_semantics=sem, vmem_limit_bytes=VMEM_LIMIT)


def _dot(a, b):
    return jnp.dot(a, b, preferred_element_type=F32)


def _dot_nt(a, b):
    return lax.dot_general(a, b, (((1,), (1,)), ((), ())), preferred_element_type=F32)


def _dot_tn(a, b):
    return lax.dot_general(a, b, (((0,), (0,)), ((), ())), preferred_element_type=F32)


def _ada_kernel(c_ref, w_ref, b_ref, o_ref):
    c = c_ref[...]
    a = c * jax.nn.sigmoid(c)
    o_ref[...] = _dot(a.astype(BF16), w_ref[...].astype(BF16)) + b_ref[...]


def _ada(c, w_ada, b_ada):
    B, D = c.shape
    N = w_ada.shape[1]
    tn = 1024
    return pl.pallas_call(
        _ada_kernel,
        grid=(N // tn,),
        in_specs=[pl.BlockSpec((B, D), lambda j: (0, 0)),
                  pl.BlockSpec((D, tn), lambda j: (0, j)),
                  pl.BlockSpec((1, tn), lambda j: (0, j))],
        out_specs=pl.BlockSpec((B, tn), lambda j: (0, j)),
        out_shape=jax.ShapeDtypeStruct((B, N), F32),
        compiler_params=_params(("parallel",)),
        name="ada",
    )(c, w_ada, b_ada.reshape(1, N))


def _proj_kernel(x_ref, sc_ref, sh_ref, g_ref, w_ref, o_ref, h_ref):
    @pl.when(pl.program_id(1) == 0)
    def _():
        x = x_ref[...]
        ms = jnp.mean(x * x, axis=-1, keepdims=True)
        y = x * lax.rsqrt(ms + NORM_EPS) * g_ref[...]
        h_ref[...] = (y * (1.0 + sc_ref[0]) + sh_ref[0]).astype(BF16)

    o_ref[...] = _dot(h_ref[...], w_ref[...]).astype(BF16)


def _proj(x2, ada3, norm_g, w_cat, S):
    M, D = x2.shape
    N = w_cat.shape[1]
    tm, tn = min(2048, S), 512
    tpb = S // tm
    return pl.pallas_call(
        _proj_kernel,
        grid=(M // tm, N // tn),
        in_specs=[pl.BlockSpec((tm, D), lambda i, j: (i, 0)),
                  pl.BlockSpec((1, 1, D), lambda i, j: (i // tpb, 0, 1)),
                  pl.BlockSpec((1, 1, D), lambda i, j: (i // tpb, 0, 0)),
                  pl.BlockSpec((1, D), lambda i, j: (0, 0)),
                  pl.BlockSpec((D, tn), lambda i, j: (0, j))],
        out_specs=pl.BlockSpec((tm, tn), lambda i, j: (i, j)),
        out_shape=jax.ShapeDtypeStruct((M, N), BF16),
        scratch_shapes=[pltpu.VMEM((tm, D), BF16)],
        compiler_params=_params(("parallel", "arbitrary")),
        name="proj",
    )(x2, ada3, ada3, norm_g.reshape(1, D), w_cat)


def _gla_kernel(q_ref, k_ref, v_ref, r_ref, misc_ref, wg_ref, bg_ref, ng_ref, o_ref, st_ref, *, n_chunks):
    C = GLA_CHUNK
    H = GLA_HEADS
    dkh = GLA_DK // H
    dvh = GLA_DV // H

    @pl.when(pl.program_id(1) == 0)
    def _():
        st_ref[...] = jnp.zeros_like(st_ref)

    wg = wg_ref[...].astype(BF16)
    bg = bg_ref[...]
    ng = ng_ref[...]
    row = lax.broadcasted_iota(I32, (C, C), 0)
    col = lax.broadcasted_iota(I32, (C, C), 1)
    causal = row >= col
    tril = jnp.where(causal, 1.0, 0.0).astype(BF16)

    st = [st_ref[h] for h in range(H)]
    for c in range(n_chunks):
        sl = slice(c * C, (c + 1) * C)
        glow = misc_ref[sl, MISC_GLOW:MISC_GLOW + GLA_GATE_RANK]
        gate = _dot(glow, wg) + bg
        la = (jnp.minimum(gate, 0.0) - jnp.log1p(jnp.exp(-jnp.abs(gate)))) * (1.0 / GLA_TAU)
        la_hi = la.astype(BF16)
        la_lo = (la - la_hi.astype(F32)).astype(BF16)
        b = _dot(tril, la_hi) + _dot(tril, la_lo)
        b_last = b[C - 1:C, :]
        q_dec = (q_ref[sl, :].astype(F32) * (dkh ** -0.5) * jnp.exp(b)).astype(BF16)
        k = k_ref[sl, :].astype(F32)
        k_inv = (k * jnp.exp(-b)).astype(BF16)
        k_end = (k * jnp.exp(b_last - b)).astype(BF16)
        decay = jnp.exp(b_last)
        v = v_ref[sl, :]
        r = r_ref[sl, :].astype(F32)
        gate_r = r * jax.nn.sigmoid(r)
        for h in range(H):
            ks = slice(h * dkh, (h + 1) * dkh)
            vs = slice(h * dvh, (h + 1) * dvh)
            attn = jnp.where(causal, _dot_nt(q_dec[:, ks], k_inv[:, ks]), 0.0)
            o = _dot(attn.astype(BF16), v[:, vs]) + _dot_nt(q_dec[:, ks], st[h].astype(BF16))
            st[h] = decay[:, ks] * st[h] + _dot_tn(v[:, vs], k_end[:, ks])
            ms = jnp.mean(o * o, axis=-1, keepdims=True)
            on = o * lax.rsqrt(ms + NORM_EPS) * ng[:, vs]
            o_ref[sl, vs] = (on * gate_r[:, vs]).astype(BF16)
    for h in range(H):
        st_ref[h] = st[h]


def _gla(proj, w_gate_up, b_gate, norm_g, B, S):
    M = proj.shape[0]
    rb = min(256, S)
    nrb = S // rb
    dkh = GLA_DK // GLA_HEADS
    dvh = GLA_DV // GLA_HEADS
    rowmap = lambda b, c: b * nrb + c
    return pl.pallas_call(
        functools.partial(_gla_kernel, n_chunks=rb // GLA_CHUNK),
        grid=(B, nrb),
        in_specs=[pl.BlockSpec((rb, GLA_DK), lambda b, c: (rowmap(b, c), COL_GQ // GLA_DK)),
                  pl.BlockSpec((rb, GLA_DK), lambda b, c: (rowmap(b, c), COL_GK // GLA_DK)),
                  pl.BlockSpec((rb, GLA_DV), lambda b, c: (rowmap(b, c), COL_GV // GLA_DV)),
                  pl.BlockSpec((rb, GLA_DV), lambda b, c: (rowmap(b, c), COL_GR // GLA_DV)),
                  pl.BlockSpec((rb, 128), lambda b, c: (rowmap(b, c), COL_MISC // 128)),
                  pl.BlockSpec((GLA_GATE_RANK, GLA_DK), lambda b, c: (0, 0)),
                  pl.BlockSpec((1, GLA_DK), lambda b, c: (0, 0)),
                  pl.BlockSpec((1, GLA_DV), lambda b, c: (0, 0))],
        out_specs=pl.BlockSpec((rb, GLA_DV), lambda b, c: (rowmap(b, c), 0)),
        out_shape=jax.ShapeDtypeStruct((M, GLA_DV), BF16),
        scratch_shapes=[pltpu.VMEM((GLA_HEADS, dvh, dkh), F32)],
        compiler_params=_params(("parallel", "arbitrary")),
        name="gla",
    )(proj, proj, proj, proj, proj, w_gate_up, b_gate.reshape(1, GLA_DK), norm_g.reshape(1, GLA_DV))


def _kvprep_kernel(kv_ref, misc_ref, g_ref, ckv_ref, ckvt_ref, ikt_ref):
    x = kv_ref[...].astype(F32)
    ms = jnp.mean(x * x, axis=-1, keepdims=True)
    c = x * lax.rsqrt(ms + NORM_EPS) * g_ref[...]
    ckv_ref[...] = c.astype(BF16)
    ckvt_ref[0] = c.T.astype(BF16)
    ikt_ref[0] = misc_ref[:, MISC_IK:MISC_IK + IDX_DIM].astype(F32).T.astype(BF16)


def _kvprep(proj, kv_norm_g, KC):
    M = proj.shape[0]
    return pl.pallas_call(
        _kvprep_kernel,
        grid=(M // KC,),
        in_specs=[pl.BlockSpec((KC, DSA_LATENT), lambda i: (i, COL_DKV // DSA_LATENT)),
                  pl.BlockSpec((KC, 128), lambda i: (i, COL_MISC // 128)),
                  pl.BlockSpec((1, DSA_LATENT), lambda i: (0, 0))],
        out_specs=[pl.BlockSpec((KC, DSA_LATENT), lambda i: (i, 0)),
                   pl.BlockSpec((1, DSA_LATENT, KC), lambda i: (i, 0, 0)),
                   pl.BlockSpec((1, IDX_DIM, KC), lambda i: (i, 0, 0))],
        out_shape=[jax.ShapeDtypeStruct((M, DSA_LATENT), BF16),
                   jax.ShapeDtypeStruct((M // KC, DSA_LATENT, KC), BF16),
                   jax.ShapeDtypeStruct((M // KC, IDX_DIM, KC), BF16)],
        compiler_params=_params(("parallel",)),
        name="kvprep",
    )(proj, proj, kv_norm_g.reshape(1, DSA_LATENT))


def _dsa_kernel(dq_ref, iq_ref, misc_ref, ikt_ref, ckv_ref, ckvt_ref, wuk_ref, wuv_ref, o_ref,
                keys_ref, qlat_ref, m_ref, l_ref, acc_ref, thr_ref, tie_ref, *, S, KC, k_sel):
    QB = Q_BLOCK
    H = DSA_HEADS
    qb = pl.program_id(1)
    nkc = (qb * QB + QB + KC - 1) // KC
    t_pos = qb * QB + lax.broadcasted_iota(I32, (QB, 1), 0)
    lane_pos = lax.broadcasted_iota(I32, (1, KC), 1)

    iw = misc_ref[:, MISC_IW:MISC_IW + IDX_HEADS].astype(F32) * (IDX_HEADS ** -0.5) * (IDX_DIM ** -0.5)
    iq = iq_ref[...]

    def score_body(kc, carry):
        k0 = pl.multiple_of(kc * KC, KC)
        ikc = ikt_ref[kc]
        score = jnp.zeros((QB, KC), F32)
        for h in range(IDX_HEADS):
            rel = _dot(iq[:, h * IDX_DIM:(h + 1) * IDX_DIM], ikc)
            rel = jnp.maximum(rel, 0.0)
            score = score + iw[:, h:h + 1] * rel
        valid = (k0 + lane_pos) <= t_pos
        keys_ref[kc] = jnp.where(valid, score, -jnp.inf)
        return carry

    lax.fori_loop(0, nkc, score_body, 0)

    def count(pred):
        def body(kc, acc):
            kv = keys_ref[kc]
            for g in range(KC // 128):
                acc = acc + jnp.where(pred(kv[:, g * 128:(g + 1) * 128], kc * KC + g * 128), 1.0, 0.0)
            return acc
        acc = lax.fori_loop(0, nkc, body, jnp.zeros((QB, 128), F32))
        return jnp.sum(acc, axis=1, keepdims=True)

    def as_float(code):
        return pltpu.bitcast(code ^ ((code >> 31) & 0x7FFFFFFF), F32)

    def bit_body(i, code):
        cand = code + (jnp.int32(1) << (31 - i))
        cand_f = as_float(cand)
        cnt = count(lambda kv, base: kv >= cand_f)
        return jnp.where(cnt >= k_sel, cand, code)

    code = lax.fori_loop(0, 32, bit_body, jnp.full((QB, 1), INT_MIN, I32))
    thr = jnp.where(code == INT_MIN, -jnp.inf, as_float(code))
    thr_ref[...] = thr
    n_gt = count(lambda kv, base: kv > thr)
    n_ge = count(lambda kv, base: kv >= thr)
    tie_ref[...] = jnp.full((QB, 1), S, I32)

    @pl.when(jnp.max(n_ge) > k_sel)
    def _():
        need = k_sel - n_gt
        lane = lax.broadcasted_iota(I32, (1, 128), 1)

        def pos_body(i, p):
            cand = p + (jnp.int32(1) << (S.bit_length() - 1 - i))
            cnt = count(lambda kv, base: (kv == thr) & ((base + lane) < cand))
            return jnp.where(cnt < need, cand, p)

        tie_ref[...] = lax.fori_loop(0, S.bit_length(), pos_body, jnp.zeros((QB, 1), I32))

    thr = thr_ref[...]
    tie = tie_ref[...]

    for h in range(H):
        ql = _dot(dq_ref[:, h * DSA_HEAD_DIM:(h + 1) * DSA_HEAD_DIM], wuk_ref[h])
        qlat_ref[h * QB:(h + 1) * QB, :] = ql.astype(BF16)
    m_ref[...] = jnp.full(m_ref.shape, MASK_NEG, F32)
    l_ref[...] = jnp.zeros(l_ref.shape, F32)
    acc_ref[...] = jnp.zeros(acc_ref.shape, F32)

    log2e = 1.4426950408889634
    c_raw = (DSA_HEAD_DIM ** -0.5) * log2e
    HG = 2
    head = lax.broadcasted_iota(I32, (H * QB, 1), 0) // QB
    slope2 = jnp.exp2((head + 1).astype(F32) * (-8.0 / H)) * log2e

    def att_body(kc, carry):
        k0 = pl.multiple_of(kc * KC, KC)
        kv = ckv_ref[pl.ds(k0, KC), :]
        keyv = keys_ref[kc]
        kpos = k0 + lane_pos
        dist = t_pos - kpos
        sel = ((keyv > thr) | ((keyv == thr) & (kpos <= tie))) & (dist >= 0)
        dmask = jnp.where(sel, dist.astype(F32), MASK_DIST)
        dmask = jnp.concatenate([dmask] * HG, axis=0)
        kvt = ckvt_ref[kc]
        for g in range(H // HG):
            rows = slice(g * HG * QB, (g + 1) * HG * QB)
            lg = _dot(qlat_ref[rows], kvt) * c_raw - slope2[rows] * dmask
            m_old = m_ref[rows]
            m_new = jnp.maximum(m_old, jnp.max(lg, axis=1, keepdims=True))
            alpha = jnp.exp2(m_old - m_new)
            p = jnp.exp2(lg - m_new)
            l_ref[rows] = alpha * l_ref[rows] + jnp.sum(p, axis=1, keepdims=True)
            acc_ref[rows] = alpha * acc_ref[rows] + _dot(p.astype(BF16), kv)
            m_ref[rows] = m_new
        return carry

    lax.fori_loop(0, nkc, att_body, 0)

    for h in range(H):
        rows = slice(h * QB, (h + 1) * QB)
        o_lat = acc_ref[rows] / l_ref[rows]
        o_ref[:, h * DSA_HEAD_DIM:(h + 1) * DSA_HEAD_DIM] = _dot(o_lat.astype(BF16), wuv_ref[h]).astype(BF16)


def _dsa_key_chunk(S):
    return min(1024, S)


def _dsa(proj, ikt, ckv, ckvt, w_uk, w_uv, B, S):
    M = proj.shape[0]
    QB = Q_BLOCK
    nqb = S // QB
    KC = _dsa_key_chunk(S)
    k_sel = min(DSA_TOPK, S // 4)
    HD = DSA_HEADS * DSA_HEAD_DIM
    return pl.pallas_call(
        functools.partial(_dsa_kernel, S=S, KC=KC, k_sel=k_sel),
        grid=(B, nqb),
        in_specs=[pl.BlockSpec((QB, HD), lambda b, q: (b * nqb + q, COL_DQ // HD)),
                  pl.BlockSpec((QB, IDX_HEADS * IDX_DIM), lambda b, q: (b * nqb + q, COL_IQ // (IDX_HEADS * IDX_DIM))),
                  pl.BlockSpec((QB, 128), lambda b, q: (b * nqb + q, COL_MISC // 128)),
                  pl.BlockSpec((S // KC, IDX_DIM, KC), lambda b, q: (b, 0, 0)),
                  pl.BlockSpec((S, DSA_LATENT), lambda b, q: (b, 0)),
                  pl.BlockSpec((S // KC, DSA_LATENT, KC), lambda b, q: (b, 0, 0)),
                  pl.BlockSpec((DSA_HEADS, DSA_HEAD_DIM, DSA_LATENT), lambda b, q: (0, 0, 0)),
                  pl.BlockSpec((DSA_HEADS, DSA_LATENT, DSA_HEAD_DIM), lambda b, q: (0, 0, 0))],
        out_specs=pl.BlockSpec((QB, HD), lambda b, q: (b * nqb + q, 0)),
        out_shape=jax.ShapeDtypeStruct((M, HD), BF16),
        scratch_shapes=[pltpu.VMEM((S // KC, QB, KC), F32),
                        pltpu.VMEM((DSA_HEADS * QB, DSA_LATENT), BF16),
                        pltpu.VMEM((DSA_HEADS * QB, 1), F32),
                        pltpu.VMEM((DSA_HEADS * QB, 1), F32),
                        pltpu.VMEM((DSA_HEADS * QB, DSA_LATENT), F32),
                        pltpu.VMEM((QB, 1), F32),
                        pltpu.VMEM((QB, 1), I32)],
        compiler_params=_params(("parallel", "arbitrary")),
        name="dsa",
    )(proj, proj, proj, ikt, ckv, ckvt, w_uk, w_uv)


def _merge_kernel(ya_ref, yb_ref, za_ref, zb_ref, x_ref, g1_ref, sc2_ref, sh2_ref, n2_ref,
                  wa_ref, wb_ref, wo_ref, wq_ref, k1_ref, k2_ref,
                  x1_ref, h2_ref, s1_ref, s2_ref):
    ya = _dot(ya_ref[...], wa_ref[...])
    yb = _dot(yb_ref[...], wb_ref[...])
    mix = jax.nn.sigmoid(za_ref[...].astype(F32)) * ya + jax.nn.sigmoid(zb_ref[...].astype(F32)) * yb
    y = _dot(mix.astype(BF16), wo_ref[...])
    x1 = x_ref[...] + g1_ref[0] * y
    x1_ref[...] = x1
    ms = jnp.mean(x1 * x1, axis=-1, keepdims=True)
    h2 = (x1 * lax.rsqrt(ms + NORM_EPS) * n2_ref[...]) * (1.0 + sc2_ref[0]) + sh2_ref[0]
    h2b = h2.astype(BF16)
    h2_ref[...] = h2b
    q = _dot(h2b, wq_ref[...]).astype(BF16)
    half = PEER_DKEY // 2
    for h in range(PEER_HEADS):
        s1_ref[h] = _dot_nt(k1_ref[...], q[:, h * PEER_DKEY:h * PEER_DKEY + half])
        s2_ref[h] = _dot_nt(k2_ref[...], q[:, h * PEER_DKEY + half:(h + 1) * PEER_DKEY])


def _merge(ya, yb, proj, x2, ada3, norm2_g, wa, wb, wo, wq, k1, k2, S):
    M, D = x2.shape
    tm = min(512, S)
    tpb = S // tm
    full = lambda shape: pl.BlockSpec(shape, lambda i: (0,) * len(shape))
    adaspec = lambda k: pl.BlockSpec((1, 1, D), lambda i: (i // tpb, 0, k))
    return pl.pallas_call(
        _merge_kernel,
        grid=(M // tm,),
        in_specs=[pl.BlockSpec((tm, D), lambda i: (i, 0)),
                  pl.BlockSpec((tm, D), lambda i: (i, 0)),
                  pl.BlockSpec((tm, D), lambda i: (i, COL_ZA // D)),
                  pl.BlockSpec((tm, D), lambda i: (i, COL_ZB // D)),
                  pl.BlockSpec((tm, D), lambda i: (i, 0)),
                  adaspec(2), adaspec(4), adaspec(3),
                  full((1, D)),
                  full((D, D)), full((D, D)), full((D, D)), full((D, PEER_HEADS * PEER_DKEY)),
                  full((PEER_NKEYS, PEER_DKEY // 2)), full((PEER_NKEYS, PEER_DKEY // 2))],
        out_specs=[pl.BlockSpec((tm, D), lambda i: (i, 0)),
                   pl.BlockSpec((tm, D), lambda i: (i, 0)),
                   pl.BlockSpec((PEER_HEADS, PEER_NKEYS, tm), lambda i: (0, 0, i)),
                   pl.BlockSpec((PEER_HEADS, PEER_NKEYS, tm), lambda i: (0, 0, i))],
        out_shape=[jax.ShapeDtypeStruct((M, D), F32),
                   jax.ShapeDtypeStruct((M, D), BF16),
                   jax.ShapeDtypeStruct((PEER_HEADS, PEER_NKEYS, M), F32),
                   jax.ShapeDtypeStruct((PEER_HEADS, PEER_NKEYS, M), F32)],
        compiler_params=_params(("parallel",)),
        name="merge",
    )(ya, yb, proj, proj, x2, ada3, ada3, ada3, norm2_g.reshape(1, D), wa, wb, wo, wq, k1, k2)


_PAIRS = [(j1, j2) for j1 in range(PEER_TOPK) for j2 in range(PEER_TOPK) if (j1 + 1) * (j2 + 1) <= PEER_TOPK]
_NCAND = -(-len(_PAIRS) // 8) * 8


def _extract_top(s, rounds):
    neg_iota = (-lax.broadcasted_iota(I32, s.shape, 0)).astype(F32)
    vals, idxs = [], []
    for _ in range(rounds):
        m = jnp.max(s, axis=0, keepdims=True)
        neg_idx = jnp.max(jnp.where(s == m, neg_iota, -jnp.inf), axis=0, keepdims=True)
        s = jnp.where(neg_iota == neg_idx, -jnp.inf, s)
        vals.append(m)
        idxs.append(-neg_idx)
    return vals, idxs


def _route_kernel(s1_ref, s2_ref, ia_ref, ib_ref, g_ref, ta_ref, tb_ref, tg_ref):
    R = s1_ref.shape[2]
    K = PEER_TOPK
    pad = _NCAND - len(_PAIRS)

    def head_body(h, carry):
        v1, i1 = _extract_top(s1_ref[h], K)
        v2, i2 = _extract_top(s2_ref[h], K)
        cand = jnp.concatenate([v1[a] + v2[b] for a, b in _PAIRS] + [jnp.full((pad, R), -jnp.inf, F32)], axis=0)
        ca = jnp.concatenate([i1[a] for a, b in _PAIRS] + [jnp.zeros((pad, R), F32)], axis=0)
        cb = jnp.concatenate([i2[b] for a, b in _PAIRS] + [jnp.zeros((pad, R), F32)], axis=0)
        best, pos = _extract_top(cand, K)
        iota = lax.broadcasted_iota(I32, cand.shape, 0).astype(F32)
        e = [jnp.exp(bv - best[0]) for bv in best]
        denom = e[0]
        for j in range(1, K):
            denom = denom + e[j]
        ia_rows, ib_rows, g_rows = [], [], []
        for j in range(K):
            hit = iota == pos[j]
            ia_rows.append(jnp.sum(jnp.where(hit, ca, 0.0), axis=0, keepdims=True))
            ib_rows.append(jnp.sum(jnp.where(hit, cb, 0.0), axis=0, keepdims=True))
            g_rows.append(e[j] / denom)
        rows = pl.ds(pl.multiple_of(h * K, K), K)
        ta_ref[rows, :] = jnp.concatenate(ia_rows, axis=0)
        tb_ref[rows, :] = jnp.concatenate(ib_rows, axis=0)
        tg_ref[rows, :] = jnp.concatenate(g_rows, axis=0)
        return carry

    lax.fori_loop(0, PEER_HEADS, head_body, 0)
    ia_ref[...] = ta_ref[...].T
    ib_ref[...] = tb_ref[...].T
    g_ref[...] = tg_ref[...].T


def _route(s1, s2):
    M = s1.shape[2]
    R = 512
    NS = PEER_HEADS * PEER_TOPK
    spec_in = pl.BlockSpec((PEER_HEADS, PEER_NKEYS, R), lambda i: (0, 0, i))
    spec_out = pl.BlockSpec((R, NS), lambda i: (i, 0))
    return pl.pallas_call(
        _route_kernel,
        grid=(M // R,),
        in_specs=[spec_in, spec_in],
        out_specs=[spec_out, spec_out, spec_out],
        out_shape=[jax.ShapeDtypeStruct((M, NS), F32)] * 3,
        scratch_shapes=[pltpu.VMEM((NS, R), F32)] * 3,
        compiler_params=_params(("parallel",)),
        name="route",
    )(s1, s2)


GM_PITCH = PEER_NKEYS // 2 + 4

def _peer_kernel(h2_ref, ia_ref, ib_ref, g_ref, u_ref, v_ref, x1_ref, g2_ref, fg_ref, o_ref,
                 gm_ref, acc_ref, *, T, AE, final_norm):
    e = pl.program_id(1)
    NK = PEER_NKEYS
    NA = NK // 2
    steps_per_build = NA // AE
    TOK_UNROLL = 64

    @pl.when(e == 0)
    def _():
        acc_ref[...] = jnp.zeros_like(acc_ref)

    @pl.when(e % steps_per_build == 0)
    def _():
        a_base = (e // steps_per_build) * NA
        sub_a = (lax.broadcasted_iota(I32, (NA, NK), 0) + a_base).astype(F32)
        sub_b = lax.broadcasted_iota(I32, (NK, NK), 0).astype(F32)

        zero = jnp.zeros((NA, NK), BF16)

        def tok_body(t8, carry):
            r0 = pl.multiple_of(t8 * TOK_UNROLL, TOK_UNROLL)
            base = pl.multiple_of(r0 * GM_PITCH, TOK_UNROLL)
            ia8 = ia_ref[pl.ds(r0, TOK_UNROLL), :]
            ib8 = ib_ref[pl.ds(r0, TOK_UNROLL), :]
            g8 = g_ref[pl.ds(r0, TOK_UNROLL), :]
            for j in range(0, TOK_UNROLL, 2):
                am = [jnp.where(sub_a == ia8[k:k + 1, :], g8[k:k + 1, :], 0.0).astype(BF16) for k in (j, j + 1)]
                bm = [jnp.where(sub_b == ib8[k:k + 1, :], 1.0, 0.0).astype(BF16) for k in (j, j + 1)]
                lhs = jnp.concatenate([jnp.concatenate([am[0], zero], axis=1),
                                       jnp.concatenate([zero, am[1]], axis=1)], axis=0)
                pair = _dot_nt(lhs, jnp.concatenate(bm, axis=1))
                gm_ref[pl.ds(base + j * GM_PITCH, NA), :] = pair[:NA]
                gm_ref[pl.ds(base + (j + 1) * GM_PITCH, NA), :] = pair[NA:]
            return carry

        lax.fori_loop(0, T // TOK_UNROLL, tok_body, 0)

    a_loc = (e % steps_per_build) * AE
    s = _dot_nt(h2_ref[...], u_ref[...])
    gt = jnp.concatenate([gm_ref[pl.ds(a_loc + j, T, stride=GM_PITCH), :] for j in range(AE)], axis=1)
    act = 0.5 * s * (1.0 + lax.erf(s * (0.5 ** 0.5)))
    acc_ref[...] += _dot((act * gt).astype(BF16), v_ref[...])

    @pl.when(e == pl.num_programs(1) - 1)
    def _():
        x2 = x1_ref[...] + g2_ref[0] * acc_ref[...]
        if final_norm:
            ms = jnp.mean(x2 * x2, axis=-1, keepdims=True)
            x2 = x2 * lax.rsqrt(ms + NORM_EPS) * fg_ref[...]
        o_ref[...] = x2


def _peer(h2, ia, ib, g, u, v, x1, ada3, final_g, S, final_norm):
    M, D = x1.shape
    T = min(512, S)
    AE = 8
    ET = AE * PEER_NKEYS
    NE = u.shape[0]
    NS = PEER_HEADS * PEER_TOPK
    tpb = S // T
    return pl.pallas_call(
        functools.partial(_peer_kernel, T=T, AE=AE, final_norm=final_norm),
        grid=(M // T, NE // ET),
        in_specs=[pl.BlockSpec((T, D), lambda i, e: (i, 0)),
                  pl.BlockSpec((T, NS), lambda i, e: (i, 0)),
                  pl.BlockSpec((T, NS), lambda i, e: (i, 0)),
                  pl.BlockSpec((T, NS), lambda i, e: (i, 0)),
                  pl.BlockSpec((ET, D), lambda i, e: (e, 0)),
                  pl.BlockSpec((ET, D), lambda i, e: (e, 0)),
                  pl.BlockSpec((T, D), lambda i, e: (i, 0)),
                  pl.BlockSpec((1, 1, D), lambda i, e: (i // tpb, 0, 5)),
                  pl.BlockSpec((1, D), lambda i, e: (0, 0))],
        out_specs=pl.BlockSpec((T, D), lambda i, e: (i, 0)),
        out_shape=jax.ShapeDtypeStruct((M, D), F32),
        scratch_shapes=[pltpu.VMEM((T * GM_PITCH, PEER_NKEYS), F32),
                        pltpu.VMEM((T, D), F32)],
        compiler_params=_params(("parallel", "arbitrary")),
        name="peer",
    )(h2, ia, ib, g, u, v, x1, ada3, final_g.reshape(1, D))


def _cat_w_in(w_in):
    sizes = (GLA_DK, GLA_DK, GLA_DV, GLA_DV, GLA_GATE_RANK, DSA_HEADS * DSA_HEAD_DIM, DSA_LATENT,
             IDX_HEADS * IDX_DIM, IDX_DIM, IDX_HEADS, D_MODEL, D_MODEL)
    offs = [0]
    for s in sizes:
        offs.append(offs[-1] + s)
    gq, gk, gv, gr, glow, dq, dkv, iq, ik, iw, za, zb = [w_in[:, offs[i]:offs[i + 1]] for i in range(len(sizes))]
    pad_misc = jnp.zeros((w_in.shape[0], 128 - (GLA_GATE_RANK + IDX_DIM + IDX_HEADS)), w_in.dtype)
    pad_end = jnp.zeros((w_in.shape[0], PROJ_COLS - (COL_MISC + 128)), w_in.dtype)
    return jnp.concatenate([gv, gr, dq, za, zb, gq, gk, iq, dkv, glow, ik, iw, pad_misc, pad_end], axis=1).astype(BF16)


def kernel(x, c, w_ada, b_ada, norm1_g, w_in, gla_w_gate_up, gla_b_gate, gla_norm_g, dsa_kv_norm_g, dsa_w_uk,
           dsa_w_uv, w_branch_a, w_branch_b, w_out, norm2_g, peer_w_q, peer_sub_keys_1, peer_sub_keys_2,
           peer_u, peer_v, final_norm_g):
    B, S, D = x.shape
    depth = w_ada.shape[0]
    x2 = x.reshape(B * S, D)
    for l in range(depth):
        ada3 = _ada(c, w_ada[l], b_ada[l]).reshape(B, 1, 6 * D)
        proj = _proj(x2, ada3, norm1_g[l], _cat_w_in(w_in[l]), S)
        ya = _gla(proj, gla_w_gate_up[l], gla_b_gate[l], gla_norm_g[l], B, S)
        ckv, ckvt, ikt = _kvprep(proj, dsa_kv_norm_g[l], _dsa_key_chunk(S))
        yb = _dsa(proj, ikt, ckv, ckvt, dsa_w_uk[l].astype(BF16), dsa_w_uv[l].astype(BF16), B, S)
        x1, h2, s1, s2 = _merge(ya, yb, proj, x2, ada3, norm2_g[l],
                                w_branch_a[l].astype(BF16), w_branch_b[l].astype(BF16), w_out[l].astype(BF16),
                                peer_w_q[l].astype(BF16), peer_sub_keys_1[l].astype(BF16),
                                peer_sub_keys_2[l].astype(BF16), S)
        ia, ib, g = _route(s1, s2)
        x2 = _peer(h2, ia, ib, g, peer_u[l].astype(BF16), peer_v[l].astype(BF16), x1, ada3, final_norm_g, S,
                   final_norm=(l == depth - 1))
    return x2.reshape(B, S, D)
```

```python
import functools

import jax
import jax.numpy as jnp
from jax import lax
from jax.experimental import pallas as pl
from jax.experimental.pallas import tpu as pltpu

F32 = jnp.float32
BF16 = jnp.bfloat16
I32 = jnp.int32

D_MODEL = 1024
GLA_HEADS = 4
GLA_DK = D_MODEL // 2
GLA_DV = D_MODEL
GLA_GATE_RANK = 16
GLA_TAU = 16.0
GLA_CHUNK = 64
DSA_HEADS = 8
DSA_HEAD_DIM = 128
DSA_LATENT = 256
IDX_HEADS = 8
IDX_DIM = 64
DSA_TOPK = 256
Q_BLOCK = 128
PEER_HEADS = 8
PEER_NKEYS = 128
PEER_DKEY = 256
PEER_TOPK = 16
NORM_EPS = 1e-6

INT_MIN = -(2 ** 31)
MASK_NEG = -1e30
MASK_DIST = 1e30

COL_GV, COL_GR, COL_DQ, COL_ZA, COL_ZB = 0, 1024, 2048, 3072, 4096
COL_GQ, COL_GK, COL_IQ, COL_DKV, COL_MISC = 5120, 5632, 6144, 6656, 6912
PROJ_COLS = 7168
MISC_GLOW, MISC_IK, MISC_IW = 0, 16, 80

VMEM_LIMIT = 56 * 1024 * 1024


def _params(sem):
    return pltpu.CompilerParams(dimension_semantics=sem, vmem_limit_bytes=VMEM_LIMIT)


def _dot(a, b):
    return jnp.dot(a, b, preferred_element_type=F32)


def _dot_nt(a, b):
    return lax.dot_general(a, b, (((1,), (1,)), ((), ())), preferred_element_type=F32)


def _dot_tn(a, b):
    return lax.dot_general(a, b, (((0,), (0,)), ((), ())), preferred_element_type=F32)


def _ada_kernel(c_ref, w_ref, b_ref, o_ref):
    c = c_ref[...]
    a = c * jax.nn.sigmoid(c)
    o_ref[...] = _dot(a.astype(BF16), w_ref[...].astype(BF16)) + b_ref[...]


def _ada(c, w_ada, b_ada):
    B, D = c.shape
    N = w_ada.shape[1]
    tn = 1024
    return pl.pallas_call(
        _ada_kernel,
        grid=(N // tn,),
        in_specs=[pl.BlockSpec((B, D), lambda j: (0, 0)),
                  pl.BlockSpec((D, tn), lambda j: (0, j)),
                  pl.BlockSpec((1, tn), lambda j: (0, j))],
        out_specs=pl.BlockSpec((B, tn), lambda j: (0, j)),
        out_shape=jax.ShapeDtypeStruct((B, N), F32),
        compiler_params=_params(("parallel",)),
        name="ada",
    )(c, w_ada, b_ada.reshape(1, N))


def _proj_kernel(x_ref, sc_ref, sh_ref, g_ref, w_ref, o_ref, h_ref):
    @pl.when(pl.program_id(1) == 0)
    def _():
        x = x_ref[...]
        ms = jnp.mean(x * x, axis=-1, keepdims=True)
        y = x * lax.rsqrt(ms + NORM_EPS) * g_ref[...]
        h_ref[...] = (y * (1.0 + sc_ref[0]) + sh_ref[0]).astype(BF16)

    o_ref[...] = _dot(h_ref[...], w_ref[...]).astype(BF16)


def _proj(x2, ada3, norm_g, w_cat, S):
    M, D = x2.shape
    N = w_cat.shape[1]
    tm, tn = min(2048, S), 1024
    tpb = S // tm
    return pl.pallas_call(
        _proj_kernel,
        grid=(M // tm, N // tn),
        in_specs=[pl.BlockSpec((tm, D), lambda i, j: (i, 0)),
                  pl.BlockSpec((1, 1, D), lambda i, j: (i // tpb, 0, 1)),
                  pl.BlockSpec((1, 1, D), lambda i, j: (i // tpb, 0, 0)),
                  pl.BlockSpec((1, D), lambda i, j: (0, 0)),
                  pl.BlockSpec((D, tn), lambda i, j: (0, j))],
        out_specs=pl.BlockSpec((tm, tn), lambda i, j: (i, j)),
        out_shape=jax.ShapeDtypeStruct((M, N), BF16),
        scratch_shapes=[pltpu.VMEM((tm, D), BF16)],
        compiler_params=_params(("parallel", "arbitrary")),
        name="proj",
    )(x2, ada3, ada3, norm_g.reshape(1, D), w_cat)


def _gla_kernel(q_ref, k_ref, v_ref, r_ref, misc_ref, wg_ref, bg_ref, ng_ref, o_ref, st_ref, *, n_chunks):
    C = GLA_CHUNK
    H = GLA_HEADS
    dkh = GLA_DK // H
    dvh = GLA_DV // H

    @pl.when(pl.program_id(1) == 0)
    def _():
        st_ref[...] = jnp.zeros_like(st_ref)

    wg = wg_ref[...].astype(BF16)
    bg = bg_ref[...]
    ng = ng_ref[...]
    row = lax.broadcasted_iota(I32, (C, C), 0)
    col = lax.broadcasted_iota(I32, (C, C), 1)
    causal = row >= col
    tril = jnp.where(causal, 1.0, 0.0).astype(BF16)

    st = [st_ref[h] for h in range(H)]
    for c in range(n_chunks):
        sl = slice(c * C, (c + 1) * C)
        glow = misc_ref[sl, MISC_GLOW:MISC_GLOW + GLA_GATE_RANK]
        gate = _dot(glow, wg) + bg
        la = (jnp.minimum(gate, 0.0) - jnp.log1p(jnp.exp(-jnp.abs(gate)))) * (1.0 / GLA_TAU)
        la_hi = la.astype(BF16)
        la_lo = (la - la_hi.astype(F32)).astype(BF16)
        b = _dot(tril, la_hi) + _dot(tril, la_lo)
        b_last = b[C - 1:C, :]
        q_dec = (q_ref[sl, :].astype(F32) * (dkh ** -0.5) * jnp.exp(b)).astype(BF16)
        k = k_ref[sl, :].astype(F32)
        k_inv = (k * jnp.exp(-b)).astype(BF16)
        k_end = (k * jnp.exp(b_last - b)).astype(BF16)
        decay = jnp.exp(b_last)
        v = v_ref[sl, :]
        r = r_ref[sl, :].astype(F32)
        gate_r = r * jax.nn.sigmoid(r)
        for h in range(H):
            ks = slice(h * dkh, (h + 1) * dkh)
            vs = slice(h * dvh, (h + 1) * dvh)
            attn = jnp.where(causal, _dot_nt(q_dec[:, ks], k_inv[:, ks]), 0.0)
            o = _dot(attn.astype(BF16), v[:, vs]) + _dot_nt(q_dec[:, ks], st[h].astype(BF16))
            st[h] = decay[:, ks] * st[h] + _dot_tn(v[:, vs], k_end[:, ks])
            ms = jnp.mean(o * o, axis=-1, keepdims=True)
            on = o * lax.rsqrt(ms + NORM_EPS) * ng[:, vs]
            o_ref[sl, vs] = (on * gate_r[:, vs]).astype(BF16)
    for h in range(H):
        st_ref[h] = st[h]


def _gla(proj, w_gate_up, b_gate, norm_g, B, S):
    M = proj.shape[0]
    rb = min(256, S)
    nrb = S // rb
    dkh = GLA_DK // GLA_HEADS
    dvh = GLA_DV // GLA_HEADS
    rowmap = lambda b, c: b * nrb + c
    return pl.pallas_call(
        functools.partial(_gla_kernel, n_chunks=rb // GLA_CHUNK),
        grid=(B, nrb),
        in_specs=[pl.BlockSpec((rb, GLA_DK), lambda b, c: (rowmap(b, c), COL_GQ // GLA_DK)),
                  pl.BlockSpec((rb, GLA_DK), lambda b, c: (rowmap(b, c), COL_GK // GLA_DK)),
                  pl.BlockSpec((rb, GLA_DV), lambda b, c: (rowmap(b, c), COL_GV // GLA_DV)),
                  pl.BlockSpec((rb, GLA_DV), lambda b, c: (rowmap(b, c), COL_GR // GLA_DV)),
                  pl.BlockSpec((rb, 128), lambda b, c: (rowmap(b, c), COL_MISC // 128)),
                  pl.BlockSpec((GLA_GATE_RANK, GLA_DK), lambda b, c: (0, 0)),
                  pl.BlockSpec((1, GLA_DK), lambda b, c: (0, 0)),
                  pl.BlockSpec((1, GLA_DV), lambda b, c: (0, 0))],
        out_specs=pl.BlockSpec((rb, GLA_DV), lambda b, c: (rowmap(b, c), 0)),
        out_shape=jax.ShapeDtypeStruct((M, GLA_DV), BF16),
        scratch_shapes=[pltpu.VMEM((GLA_HEADS, dvh, dkh), F32)],
        compiler_params=_params(("parallel", "arbitrary")),
        name="gla",
    )(proj, proj, proj, proj, proj, w_gate_up, b_gate.reshape(1, GLA_DK), norm_g.reshape(1, GLA_DV))


def _kvprep_kernel(kv_ref, misc_ref, g_ref, ckv_ref, ckvt_ref, ikt_ref):
    x = kv_ref[...].astype(F32)
    ms = jnp.mean(x * x, axis=-1, keepdims=True)
    c = x * lax.rsqrt(ms + NORM_EPS) * g_ref[...]
    ckv_ref[...] = c.astype(BF16)
    ckvt_ref[0] = c.T.astype(BF16)
    ikt_ref[0] = misc_ref[:, MISC_IK:MISC_IK + IDX_DIM].astype(F32).T.astype(BF16)


def _kvprep(proj, kv_norm_g, KC):
    M = proj.shape[0]
    return pl.pallas_call(
        _kvprep_kernel,
        grid=(M // KC,),
        in_specs=[pl.BlockSpec((KC, DSA_LATENT), lambda i: (i, COL_DKV // DSA_LATENT)),
                  pl.BlockSpec((KC, 128), lambda i: (i, COL_MISC // 128)),
                  pl.BlockSpec((1, DSA_LATENT), lambda i: (0, 0))],
        out_specs=[pl.BlockSpec((KC, DSA_LATENT), lambda i: (i, 0)),
                   pl.BlockSpec((1, DSA_LATENT, KC), lambda i: (i, 0, 0)),
                   pl.BlockSpec((1, IDX_DIM, KC), lambda i: (i, 0, 0))],
        out_shape=[jax.ShapeDtypeStruct((M, DSA_LATENT), BF16),
                   jax.ShapeDtypeStruct((M // KC, DSA_LATENT, KC), BF16),
                   jax.ShapeDtypeStruct((M // KC, IDX_DIM, KC), BF16)],
        compiler_params=_params(("parallel",)),
        name="kvprep",
    )(proj, proj, kv_norm_g.reshape(1, DSA_LATENT))


def _dsa_kernel(dq_ref, iq_ref, misc_ref, ikt_ref, ckv_ref, ckvt_ref, wuk_ref, wuv_ref, o_ref,
                keys_ref, qlat_ref, m_ref, l_ref, acc_ref, thr_ref, tie_ref, *, S, KC, k_sel):
    QB = Q_BLOCK
    H = DSA_HEADS
    qb = pl.program_id(1)
    nkc = (qb * QB + QB + KC - 1) // KC
    t_pos = qb * QB + lax.broadcasted_iota(I32, (QB, 1), 0)
    lane_pos = lax.broadcasted_iota(I32, (1, KC), 1)

    iw = misc_ref[:, MISC_IW:MISC_IW + IDX_HEADS].astype(F32) * (IDX_HEADS ** -0.5) * (IDX_DIM ** -0.5)
    iq = iq_ref[...]

    def score_body(kc, carry):
        k0 = pl.multiple_of(kc * KC, KC)
        ikc = ikt_ref[kc]
        score = jnp.zeros((QB, KC), F32)
        for h in range(IDX_HEADS):
            rel = _dot(iq[:, h * IDX_DIM:(h + 1) * IDX_DIM], ikc)
            rel = jnp.maximum(rel, 0.0)
            score = score + iw[:, h:h + 1] * rel
        valid = (k0 + lane_pos) <= t_pos
        keys_ref[kc] = jnp.where(valid, score, -jnp.inf)
        return carry

    lax.fori_loop(0, nkc, score_body, 0)

    def count(pred):
        def body(kc, acc):
            kv = keys_ref[kc]
            for g in range(KC // 128):
                acc = acc + jnp.where(pred(kv[:, g * 128:(g + 1) * 128], kc * KC + g * 128), 1.0, 0.0)
            return acc
        acc = lax.fori_loop(0, nkc, body, jnp.zeros((QB, 128), F32))
        return jnp.sum(acc, axis=1, keepdims=True)

    def as_float(code):
        return pltpu.bitcast(code ^ ((code >> 31) & 0x7FFFFFFF), F32)

    def bit_body(i, code):
        cand = code + (jnp.int32(1) << (31 - i))
        cand_f = as_float(cand)
        cnt = count(lambda kv, base: kv >= cand_f)
        return jnp.where(cnt >= k_sel, cand, code)

    code = lax.fori_loop(0, 32, bit_body, jnp.full((QB, 1), INT_MIN, I32))
    thr = jnp.where(code == INT_MIN, -jnp.inf, as_float(code))
    thr_ref[...] = thr
    n_gt = count(lambda kv, base: kv > thr)
    n_ge = count(lambda kv, base: kv >= thr)
    tie_ref[...] = jnp.full((QB, 1), S, I32)

    @pl.when(jnp.max(n_ge) > k_sel)
    def _():
        need = k_sel - n_gt
        lane = lax.broadcasted_iota(I32, (1, 128), 1)

        def pos_body(i, p):
            cand = p + (jnp.int32(1) << (S.bit_length() - 1 - i))
            cnt = count(lambda kv, base: (kv == thr) & ((base + lane) < cand))
            return jnp.where(cnt < need, cand, p)

        tie_ref[...] = lax.fori_loop(0, S.bit_length(), pos_body, jnp.zeros((QB, 1), I32))

    thr = thr_ref[...]
    tie = tie_ref[...]

    for h in range(H):
        ql = _dot(dq_ref[:, h * DSA_HEAD_DIM:(h + 1) * DSA_HEAD_DIM], wuk_ref[h])
        qlat_ref[h * QB:(h + 1) * QB, :] = ql.astype(BF16)
    m_ref[...] = jnp.full(m_ref.shape, MASK_NEG, F32)
    l_ref[...] = jnp.zeros(l_ref.shape, F32)
    acc_ref[...] = jnp.zeros(acc_ref.shape, F32)

    log2e = 1.4426950408889634
    c_raw = (DSA_HEAD_DIM ** -0.5) * log2e
    HG = 2
    head = lax.broadcasted_iota(I32, (H * QB, 1), 0) // QB
    slope2 = jnp.exp2((head + 1).astype(F32) * (-8.0 / H)) * log2e

    def att_body(kc, carry):
        k0 = pl.multiple_of(kc * KC, KC)
        kv = ckv_ref[pl.ds(k0, KC), :]
        keyv = keys_ref[kc]
        kpos = k0 + lane_pos
        dist = t_pos - kpos
        sel = ((keyv > thr) | ((keyv == thr) & (kpos <= tie))) & (dist >= 0)
        dmask = jnp.where(sel, dist.astype(F32), MASK_DIST)
        dmask = jnp.concatenate([dmask] * HG, axis=0)
        kvt = ckvt_ref[kc]
        for g in range(H // HG):
            rows = slice(g * HG * QB, (g + 1) * HG * QB)
            lg = _dot(qlat_ref[rows], kvt) * c_raw - slope2[rows] * dmask
            m_old = m_ref[rows]
            m_new = jnp.maximum(m_old, jnp.max(lg, axis=1, keepdims=True))
            alpha = jnp.exp2(m_old - m_new)
            p = jnp.exp2(lg - m_new)
            l_ref[rows] = alpha * l_ref[rows] + jnp.sum(p, axis=1, keepdims=True)
            acc_ref[rows] = alpha * acc_ref[rows] + _dot(p.astype(BF16), kv)
            m_ref[rows] = m_new
        return carry

    lax.fori_loop(0, nkc, att_body, 0)

    for h in range(H):
        rows = slice(h * QB, (h + 1) * QB)
        o_lat = acc_ref[rows] / l_ref[rows]
        o_ref[:, h * DSA_HEAD_DIM:(h + 1) * DSA_HEAD_DIM] = _dot(o_lat.astype(BF16), wuv_ref[h]).astype(BF16)


def _dsa_key_chunk(S):
    return min(1024, S)


def _dsa(proj, ikt, ckv, ckvt, w_uk, w_uv, B, S):
    M = proj.shape[0]
    QB = Q_BLOCK
    nqb = S // QB
    KC = _dsa_key_chunk(S)
    k_sel = min(DSA_TOPK, S // 4)
    HD = DSA_HEADS * DSA_HEAD_DIM
    return pl.pallas_call(
        functools.partial(_dsa_kernel, S=S, KC=KC, k_sel=k_sel),
        grid=(B, nqb),
        in_specs=[pl.BlockSpec((QB, HD), lambda b, q: (b * nqb + q, COL_DQ // HD)),
                  pl.BlockSpec((QB, IDX_HEADS * IDX_DIM), lambda b, q: (b * nqb + q, COL_IQ // (IDX_HEADS * IDX_DIM))),
                  pl.BlockSpec((QB, 128), lambda b, q: (b * nqb + q, COL_MISC // 128)),
                  pl.BlockSpec((S // KC, IDX_DIM, KC), lambda b, q: (b, 0, 0)),
                  pl.BlockSpec((S, DSA_LATENT), lambda b, q: (b, 0)),
                  pl.BlockSpec((S // KC, DSA_LATENT, KC), lambda b, q: (b, 0, 0)),
                  pl.BlockSpec((DSA_HEADS, DSA_HEAD_DIM, DSA_LATENT), lambda b, q: (0, 0, 0)),
                  pl.BlockSpec((DSA_HEADS, DSA_LATENT, DSA_HEAD_DIM), lambda b, q: (0, 0, 0))],
        out_specs=pl.BlockSpec((QB, HD), lambda b, q: (b * nqb + q, 0)),
        out_shape=jax.ShapeDtypeStruct((M, HD), BF16),
        scratch_shapes=[pltpu.VMEM((S // KC, QB, KC), F32),
                        pltpu.VMEM((DSA_HEADS * QB, DSA_LATENT), BF16),
                        pltpu.VMEM((DSA_HEADS * QB, 1), F32),
                        pltpu.VMEM((DSA_HEADS * QB, 1), F32),
                        pltpu.VMEM((DSA_HEADS * QB, DSA_LATENT), F32),
                        pltpu.VMEM((QB, 1), F32),
                        pltpu.VMEM((QB, 1), I32)],
        compiler_params=_params(("parallel", "arbitrary")),
        name="dsa",
    )(proj, proj, proj, ikt, ckv, ckvt, w_uk, w_uv)


def _merge_kernel(ya_ref, yb_ref, za_ref, zb_ref, x_ref, g1_ref, sc2_ref, sh2_ref, n2_ref,
                  wa_ref, wb_ref, wo_ref, wq_ref, k1_ref, k2_ref,
                  x1_ref, h2_ref, s1_ref, s2_ref):
    ya = _dot(ya_ref[...], wa_ref[...])
    yb = _dot(yb_ref[...], wb_ref[...])
    mix = jax.nn.sigmoid(za_ref[...].astype(F32)) * ya + jax.nn.sigmoid(zb_ref[...].astype(F32)) * yb
    y = _dot(mix.astype(BF16), wo_ref[...])
    x1 = x_ref[...] + g1_ref[0] * y
    x1_ref[...] = x1
    ms = jnp.mean(x1 * x1, axis=-1, keepdims=True)
    h2 = (x1 * lax.rsqrt(ms + NORM_EPS) * n2_ref[...]) * (1.0 + sc2_ref[0]) + sh2_ref[0]
    h2b = h2.astype(BF16)
    h2_ref[...] = h2b
    q = _dot(h2b, wq_ref[...]).astype(BF16)
    half = PEER_DKEY // 2
    for h in range(PEER_HEADS):
        s1_ref[h] = _dot_nt(k1_ref[...], q[:, h * PEER_DKEY:h * PEER_DKEY + half])
        s2_ref[h] = _dot_nt(k2_ref[...], q[:, h * PEER_DKEY + half:(h + 1) * PEER_DKEY])


def _merge(ya, yb, proj, x2, ada3, norm2_g, wa, wb, wo, wq, k1, k2, S):
    M, D = x2.shape
    tm = min(512, S)
    tpb = S // tm
    full = lambda shape: pl.BlockSpec(shape, lambda i: (0,) * len(shape))
    adaspec = lambda k: pl.BlockSpec((1, 1, D), lambda i: (i // tpb, 0, k))
    return pl.pallas_call(
        _merge_kernel,
        grid=(M // tm,),
        in_specs=[pl.BlockSpec((tm, D), lambda i: (i, 0)),
                  pl.BlockSpec((tm, D), lambda i: (i, 0)),
                  pl.BlockSpec((tm, D), lambda i: (i, COL_ZA // D)),
                  pl.BlockSpec((tm, D), lambda i: (i, COL_ZB // D)),
                  pl.BlockSpec((tm, D), lambda i: (i, 0)),
                  adaspec(2), adaspec(4), adaspec(3),
                  full((1, D)),
                  full((D, D)), full((D, D)), full((D, D)), full((D, PEER_HEADS * PEER_DKEY)),
                  full((PEER_NKEYS, PEER_DKEY // 2)), full((PEER_NKEYS, PEER_DKEY // 2))],
        out_specs=[pl.BlockSpec((tm, D), lambda i: (i, 0)),
                   pl.BlockSpec((tm, D), lambda i: (i, 0)),
                   pl.BlockSpec((PEER_HEADS, PEER_NKEYS, tm), lambda i: (0, 0, i)),
                   pl.BlockSpec((PEER_HEADS, PEER_NKEYS, tm), lambda i: (0, 0, i))],
        out_shape=[jax.ShapeDtypeStruct((M, D), F32),
                   jax.ShapeDtypeStruct((M, D), BF16),
                   jax.ShapeDtypeStruct((PEER_HEADS, PEER_NKEYS, M), F32),
                   jax.ShapeDtypeStruct((PEER_HEADS, PEER_NKEYS, M), F32)],
        compiler_params=_params(("parallel",)),
        name="merge",
    )(ya, yb, proj, proj, x2, ada3, ada3, ada3, norm2_g.reshape(1, D), wa, wb, wo, wq, k1, k2)


_PAIRS = [(j1, j2) for j1 in range(PEER_TOPK) for j2 in range(PEER_TOPK) if (j1 + 1) * (j2 + 1) <= PEER_TOPK]
_NCAND = -(-len(_PAIRS) // 8) * 8


def _extract_top(s, rounds):
    neg_iota = (-lax.broadcasted_iota(I32, s.shape, 0)).astype(F32)
    vals, idxs = [], []
    for _ in range(rounds):
        m = jnp.max(s, axis=0, keepdims=True)
        neg_idx = jnp.max(jnp.where(s == m, neg_iota, -jnp.inf), axis=0, keepdims=True)
        s = jnp.where(neg_iota == neg_idx, -jnp.inf, s)
        vals.append(m)
        idxs.append(-neg_idx)
    return vals, idxs


def _route_kernel(s1_ref, s2_ref, ia_ref, ib_ref, g_ref, ta_ref, tb_ref, tg_ref):
    R = s1_ref.shape[2]
    K = PEER_TOPK
    pad = _NCAND - len(_PAIRS)

    def head_body(h, carry):
        v1, i1 = _extract_top(s1_ref[h], K)
        v2, i2 = _extract_top(s2_ref[h], K)
        cand = jnp.concatenate([v1[a] + v2[b] for a, b in _PAIRS] + [jnp.full((pad, R), -jnp.inf, F32)], axis=0)
        ca = jnp.concatenate([i1[a] for a, b in _PAIRS] + [jnp.zeros((pad, R), F32)], axis=0)
        cb = jnp.concatenate([i2[b] for a, b in _PAIRS] + [jnp.zeros((pad, R), F32)], axis=0)
        best, pos = _extract_top(cand, K)
        iota = lax.broadcasted_iota(I32, cand.shape, 0).astype(F32)
        e = [jnp.exp(bv - best[0]) for bv in best]
        denom = e[0]
        for j in range(1, K):
            denom = denom + e[j]
        ia_rows, ib_rows, g_rows = [], [], []
        for j in range(K):
            hit = iota == pos[j]
            ia_rows.append(jnp.sum(jnp.where(hit, ca, 0.0), axis=0, keepdims=True))
            ib_rows.append(jnp.sum(jnp.where(hit, cb, 0.0), axis=0, keepdims=True))
            g_rows.append(e[j] / denom)
        rows = pl.ds(pl.multiple_of(h * K, K), K)
        ta_ref[rows, :] = jnp.concatenate(ia_rows, axis=0)
        tb_ref[rows, :] = jnp.concatenate(ib_rows, axis=0)
        tg_ref[rows, :] = jnp.concatenate(g_rows, axis=0)
        return carry

    lax.fori_loop(0, PEER_HEADS, head_body, 0)
    ia_ref[...] = ta_ref[...].T
    ib_ref[...] = tb_ref[...].T
    g_ref[...] = tg_ref[...].T


def _route(s1, s2):
    M = s1.shape[2]
    R = 512
    NS = PEER_HEADS * PEER_TOPK
    spec_in = pl.BlockSpec((PEER_HEADS, PEER_NKEYS, R), lambda i: (0, 0, i))
    spec_out = pl.BlockSpec((R, NS), lambda i: (i, 0))
    return pl.pallas_call(
        _route_kernel,
        grid=(M // R,),
        in_specs=[spec_in, spec_in],
        out_specs=[spec_out, spec_out, spec_out],
        out_shape=[jax.ShapeDtypeStruct((M, NS), F32)] * 3,
        scratch_shapes=[pltpu.VMEM((NS, R), F32)] * 3,
        compiler_params=_params(("parallel",)),
        name="route",
    )(s1, s2)


GM_PITCH = PEER_NKEYS // 2 + 4

def _peer_kernel(h2_ref, ia_ref, ib_ref, g_ref, u_ref, v_ref, x1_ref, g2_ref, fg_ref, o_ref,
                 gm_ref, acc_ref, *, T, AE, final_norm):
    e = pl.program_id(1)
    NK = PEER_NKEYS
    NA = NK // 2
    steps_per_build = NA // AE
    TOK_UNROLL = 128

    @pl.when(e == 0)
    def _():
        acc_ref[...] = jnp.zeros_like(acc_ref)

    @pl.when(e % steps_per_build == 0)
    def _():
        a_base = (e // steps_per_build) * NA
        sub_a = (lax.broadcasted_iota(I32, (NA, NK), 0) + a_base).astype(F32)
        sub_b = lax.broadcasted_iota(I32, (NK, NK), 0).astype(F32)

        zero = jnp.zeros((NA, NK), BF16)

        def tok_body(t8, carry):
            r0 = pl.multiple_of(t8 * TOK_UNROLL, TOK_UNROLL)
            base = pl.multiple_of(r0 * GM_PITCH, TOK_UNROLL)
            ia8 = ia_ref[pl.ds(r0, TOK_UNROLL), :]
            ib8 = ib_ref[pl.ds(r0, TOK_UNROLL), :]
            g8 = g_ref[pl.ds(r0, TOK_UNROLL), :]
            for j in range(0, TOK_UNROLL, 2):
                am = [jnp.where(sub_a == ia8[k:k + 1, :], g8[k:k + 1, :], 0.0).astype(BF16) for k in (j, j + 1)]
                bm = [jnp.where(sub_b == ib8[k:k + 1, :], 1.0, 0.0).astype(BF16) for k in (j, j + 1)]
                lhs = jnp.concatenate([jnp.concatenate([am[0], zero], axis=1),
                                       jnp.concatenate([zero, am[1]], axis=1)], axis=0)
                pair = _dot_nt(lhs, jnp.concatenate(bm, axis=1))
                gm_ref[pl.ds(base + j * GM_PITCH, NA), :] = pair[:NA]
                gm_ref[pl.ds(base + (j + 1) * GM_PITCH, NA), :] = pair[NA:]
            return carry

        lax.fori_loop(0, T // TOK_UNROLL, tok_body, 0)

    a_loc = (e % steps_per_build) * AE
    s = _dot_nt(h2_ref[...], u_ref[...])
    gt = jnp.concatenate([gm_ref[pl.ds(a_loc + j, T, stride=GM_PITCH), :] for j in range(AE)], axis=1)
    act = 0.5 * s * (1.0 + lax.erf(s * (0.5 ** 0.5)))
    acc_ref[...] += _dot((act * gt).astype(BF16), v_ref[...])

    @pl.when(e == pl.num_programs(1) - 1)
    def _():
        x2 = x1_ref[...] + g2_ref[0] * acc_ref[...]
        if final_norm:
            ms = jnp.mean(x2 * x2, axis=-1, keepdims=True)
            x2 = x2 * lax.rsqrt(ms + NORM_EPS) * fg_ref[...]
        o_ref[...] = x2


def _peer(h2, ia, ib, g, u, v, x1, ada3, final_g, S, final_norm):
    M, D = x1.shape
    T = min(512, S)
    AE = 8
    ET = AE * PEER_NKEYS
    NE = u.shape[0]
    NS = PEER_HEADS * PEER_TOPK
    tpb = S // T
    return pl.pallas_call(
        functools.partial(_peer_kernel, T=T, AE=AE, final_norm=final_norm),
        grid=(M // T, NE // ET),
        in_specs=[pl.BlockSpec((T, D), lambda i, e: (i, 0)),
                  pl.BlockSpec((T, NS), lambda i, e: (i, 0)),
                  pl.BlockSpec((T, NS), lambda i, e: (i, 0)),
                  pl.BlockSpec((T, NS), lambda i, e: (i, 0)),
                  pl.BlockSpec((ET, D), lambda i, e: (e, 0)),
                  pl.BlockSpec((ET, D), lambda i, e: (e, 0)),
                  pl.BlockSpec((T, D), lambda i, e: (i, 0)),
                  pl.BlockSpec((1, 1, D), lambda i, e: (i // tpb, 0, 5)),
                  pl.BlockSpec((1, D), lambda i, e: (0, 0))],
        out_specs=pl.BlockSpec((T, D), lambda i, e: (i, 0)),
        out_shape=jax.ShapeDtypeStruct((M, D), F32),
        scratch_shapes=[pltpu.VMEM((T * GM_PITCH, PEER_NKEYS), F32),
                        pltpu.VMEM((T, D), F32)],
        compiler_params=_params(("parallel", "arbitrary")),
        name="peer",
    )(h2, ia, ib, g, u, v, x1, ada3, final_g.reshape(1, D))


def _cat_w_in(w_in):
    sizes = (GLA_DK, GLA_DK, GLA_DV, GLA_DV, GLA_GATE_RANK, DSA_HEADS * DSA_HEAD_DIM, DSA_LATENT,
             IDX_HEADS * IDX_DIM, IDX_DIM, IDX_HEADS, D_MODEL, D_MODEL)
    offs = [0]
    for s in sizes:
        offs.append(offs[-1] + s)
    gq, gk, gv, gr, glow, dq, dkv, iq, ik, iw, za, zb = [w_in[:, offs[i]:offs[i + 1]] for i in range(len(sizes))]
    pad_misc = jnp.zeros((w_in.shape[0], 128 - (GLA_GATE_RANK + IDX_DIM + IDX_HEADS)), w_in.dtype)
    pad_end = jnp.zeros((w_in.shape[0], PROJ_COLS - (COL_MISC + 128)), w_in.dtype)
    return jnp.concatenate([gv, gr, dq, za, zb, gq, gk, iq, dkv, glow, ik, iw, pad_misc, pad_end], axis=1).astype(BF16)


def kernel(x, c, w_ada, b_ada, norm1_g, w_in, gla_w_gate_up, gla_b_gate, gla_norm_g, dsa_kv_norm_g, dsa_w_uk,
           dsa_w_uv, w_branch_a, w_branch_b, w_out, norm2_g, peer_w_q, peer_sub_keys_1, peer_sub_keys_2,
           peer_u, peer_v, final_norm_g):
    B, S, D = x.shape
    depth = w_ada.shape[0]
    x2 = x.reshape(B * S, D)
    for l in range(depth):
        ada3 = _ada(c, w_ada[l], b_ada[l]).reshape(B, 1, 6 * D)
        proj = _proj(x2, ada3, norm1_g[l], _cat_w_in(w_in[l]), S)
        ya = _gla(proj, gla_w_gate_up[l], gla_b_gate[l], gla_norm_g[l], B, S)
        ckv, ckvt, ikt = _kvprep(proj, dsa_kv_norm_g[l], _dsa_key_chunk(S))
        yb = _dsa(proj, ikt, ckv, ckvt, dsa_w_uk[l].astype(BF16), dsa_w_uv[l].astype(BF16), B, S)
        x1, h2, s1, s2 = _merge(ya, yb, proj, x2, ada3, norm2_g[l],
                                w_branch_a[l].astype(BF16), w_branch_b[l].astype(BF16), w_out[l].astype(BF16),
                                peer_w_q[l].astype(BF16), peer_sub_keys_1[l].astype(BF16),
                                peer_sub_keys_2[l].astype(BF16), S)
        ia, ib, g = _route(s1, s2)
        x2 = _peer(h2, ia, ib, g, peer_u[l].astype(BF16), peer_v[l].astype(BF16), x1, ada3, final_norm_g, S,
                   final_norm=(l == depth - 1))
    return x2.reshape(B, S, D)
```

```python
import functools

import jax
import jax.numpy as jnp
from jax import lax
from jax.experimental import pallas as pl
from jax.experimental.pallas import tpu as pltpu

F32 = jnp.float32
BF16 = jnp.bfloat16
I32 = jnp.int32

D_MODEL = 1024
GLA_HEADS = 4
GLA_DK = D_MODEL // 2
GLA_DV = D_MODEL
GLA_GATE_RANK = 16
GLA_TAU = 16.0
GLA_CHUNK = 64
DSA_HEADS = 8
DSA_HEAD_DIM = 128
DSA_LATENT = 256
IDX_HEADS = 8
IDX_DIM = 64
DSA_TOPK = 256
Q_BLOCK = 128
PEER_HEADS = 8
PEER_NKEYS = 128
PEER_DKEY = 256
PEER_TOPK = 16
NORM_EPS = 1e-6

INT_MIN = -(2 ** 31)
MASK_NEG = -1e30
MASK_DIST = 1e30

COL_GV, COL_GR, COL_DQ, COL_ZA, COL_ZB = 0, 1024, 2048, 3072, 4096
COL_GQ, COL_GK, COL_IQ, COL_DKV, COL_MISC = 5120, 5632, 6144, 6656, 6912
PROJ_COLS = 7168
MISC_GLOW, MISC_IK, MISC_IW = 0, 16, 80

VMEM_LIMIT = 56 * 1024 * 1024


def _params(sem):
    return pltpu.CompilerParams(dimension_semantics=sem, vmem_limit_bytes=VMEM_LIMIT)


def _dot(a, b):
    return jnp.dot(a, b, preferred_element_type=F32)


def _dot_nt(a, b):
    return lax.dot_general(a, b, (((1,), (1,)), ((), ())), preferred_element_type=F32)


def _dot_tn(a, b):
    return lax.dot_general(a, b, (((0,), (0,)), ((), ())), preferred_element_type=F32)


def _ada_kernel(c_ref, w_ref, b_ref, o_ref):
    c = c_ref[...]
    a = c * jax.nn.sigmoid(c)
    o_ref[...] = _dot(a.astype(BF16), w_ref[...].astype(BF16)) + b_ref[...]


def _ada(c, w_ada, b_ada):
    B, D = c.shape
    N = w_ada.shape[1]
    tn = 1024
    return pl.pallas_call(
        _ada_kernel,
        grid=(N // tn,),
        in_specs=[pl.BlockSpec((B, D), lambda j: (0, 0)),
                  pl.BlockSpec((D, tn), lambda j: (0, j)),
                  pl.BlockSpec((1, tn), lambda j: (0, j))],
        out_specs=pl.BlockSpec((B, tn), lambda j: (0, j)),
        out_shape=jax.ShapeDtypeStruct((B, N), F32),
        compiler_params=_params(("parallel",)),
        name="ada",
    )(c, w_ada, b_ada.reshape(1, N))


def _proj_kernel(x_ref, sc_ref, sh_ref, g_ref, w_ref, o_ref, h_ref):
    @pl.when(pl.program_id(1) == 0)
    def _():
        x = x_ref[...]
        ms = jnp.mean(x * x, axis=-1, keepdims=True)
        y = x * lax.rsqrt(ms + NORM_EPS) * g_ref[...]
        h_ref[...] = (y * (1.0 + sc_ref[0]) + sh_ref[0]).astype(BF16)

    o_ref[...] = _dot(h_ref[...], w_ref[...]).astype(BF16)


def _proj(x2, ada3, norm_g, w_cat, S):
    M, D = x2.shape
    N = w_cat.shape[1]
    tm, tn = min(2048, S), 1024
    tpb = S // tm
    return pl.pallas_call(
        _proj_kernel,
        grid=(M // tm, N // tn),
        in_specs=[pl.BlockSpec((tm, D), lambda i, j: (i, 0)),
                  pl.BlockSpec((1, 1, D), lambda i, j: (i // tpb, 0, 1)),
                  pl.BlockSpec((1, 1, D), lambda i, j: (i // tpb, 0, 0)),
                  pl.BlockSpec((1, D), lambda i, j: (0, 0)),
                  pl.BlockSpec((D, tn), lambda i, j: (0, j))],
        out_specs=pl.BlockSpec((tm, tn), lambda i, j: (i, j)),
        out_shape=jax.ShapeDtypeStruct((M, N), BF16),
        scratch_shapes=[pltpu.VMEM((tm, D), BF16)],
        compiler_params=_params(("parallel", "arbitrary")),
        name="proj",
    )(x2, ada3, ada3, norm_g.reshape(1, D), w_cat)


def _gla_kernel(q_ref, k_ref, v_ref, r_ref, misc_ref, wg_ref, bg_ref, ng_ref, o_ref, st_ref, *, n_chunks):
    C = GLA_CHUNK
    H = GLA_HEADS
    dkh = GLA_DK // H
    dvh = GLA_DV // H

    @pl.when(pl.program_id(1) == 0)
    def _():
        st_ref[...] = jnp.zeros_like(st_ref)

    wg = wg_ref[...].astype(BF16)
    bg = bg_ref[...]
    ng = ng_ref[...]
    row = lax.broadcasted_iota(I32, (C, C), 0)
    col = lax.broadcasted_iota(I32, (C, C), 1)
    causal = row >= col
    tril = jnp.where(causal, 1.0, 0.0).astype(BF16)

    st = [st_ref[h] for h in range(H)]
    for c in range(n_chunks):
        sl = slice(c * C, (c + 1) * C)
        glow = misc_ref[sl, MISC_GLOW:MISC_GLOW + GLA_GATE_RANK]
        gate = _dot(glow, wg) + bg
        la = (jnp.minimum(gate, 0.0) - jnp.log1p(jnp.exp(-jnp.abs(gate)))) * (1.0 / GLA_TAU)
        la_hi = la.astype(BF16)
        la_lo = (la - la_hi.astype(F32)).astype(BF16)
        b = _dot(tril, la_hi) + _dot(tril, la_lo)
        b_last = b[C - 1:C, :]
        q_dec = (q_ref[sl, :].astype(F32) * (dkh ** -0.5) * jnp.exp(b)).astype(BF16)
        k = k_ref[sl, :].astype(F32)
        k_inv = (k * jnp.exp(-b)).astype(BF16)
        k_end = (k * jnp.exp(b_last - b)).astype(BF16)
        decay = jnp.exp(b_last)
        v = v_ref[sl, :]
        r = r_ref[sl, :].astype(F32)
        gate_r = r * jax.nn.sigmoid(r)
        for h in range(H):
            ks = slice(h * dkh, (h + 1) * dkh)
            vs = slice(h * dvh, (h + 1) * dvh)
            attn = jnp.where(causal, _dot_nt(q_dec[:, ks], k_inv[:, ks]), 0.0)
            o = _dot(attn.astype(BF16), v[:, vs]) + _dot_nt(q_dec[:, ks], st[h].astype(BF16))
            st[h] = decay[:, ks] * st[h] + _dot_tn(v[:, vs], k_end[:, ks])
            ms = jnp.mean(o * o, axis=-1, keepdims=True)
            on = o * lax.rsqrt(ms + NORM_EPS) * ng[:, vs]
            o_ref[sl, vs] = (on * gate_r[:, vs]).astype(BF16)
    for h in range(H):
        st_ref[h] = st[h]


def _gla(proj, w_gate_up, b_gate, norm_g, B, S):
    M = proj.shape[0]
    rb = min(256, S)
    nrb = S // rb
    dkh = GLA_DK // GLA_HEADS
    dvh = GLA_DV // GLA_HEADS
    rowmap = lambda b, c: b * nrb + c
    return pl.pallas_call(
        functools.partial(_gla_kernel, n_chunks=rb // GLA_CHUNK),
        grid=(B, nrb),
        in_specs=[pl.BlockSpec((rb, GLA_DK), lambda b, c: (rowmap(b, c), COL_GQ // GLA_DK)),
                  pl.BlockSpec((rb, GLA_DK), lambda b, c: (rowmap(b, c), COL_GK // GLA_DK)),
                  pl.BlockSpec((rb, GLA_DV), lambda b, c: (rowmap(b, c), COL_GV // GLA_DV)),
                  pl.BlockSpec((rb, GLA_DV), lambda b, c: (rowmap(b, c), COL_GR // GLA_DV)),
                  pl.BlockSpec((rb, 128), lambda b, c: (rowmap(b, c), COL_MISC // 128)),
                  pl.BlockSpec((GLA_GATE_RANK, GLA_DK), lambda b, c: (0, 0)),
                  pl.BlockSpec((1, GLA_DK), lambda b, c: (0, 0)),
                  pl.BlockSpec((1, GLA_DV), lambda b, c: (0, 0))],
        out_specs=pl.BlockSpec((rb, GLA_DV), lambda b, c: (rowmap(b, c), 0)),
        out_shape=jax.ShapeDtypeStruct((M, GLA_DV), BF16),
        scratch_shapes=[pltpu.VMEM((GLA_HEADS, dvh, dkh), F32)],
        compiler_params=_params(("parallel", "arbitrary")),
        name="gla",
    )(proj, proj, proj, proj, proj, w_gate_up, b_gate.reshape(1, GLA_DK), norm_g.reshape(1, GLA_DV))


def _kvprep_kernel(kv_ref, misc_ref, g_ref, ckv_ref, ckvt_ref, ikt_ref):
    x = kv_ref[...].astype(F32)
    ms = jnp.mean(x * x, axis=-1, keepdims=True)
    c = x * lax.rsqrt(ms + NORM_EPS) * g_ref[...]
    ckv_ref[...] = c.astype(BF16)
    ckvt_ref[0] = c.T.astype(BF16)
    ikt_ref[0] = misc_ref[:, MISC_IK:MISC_IK + IDX_DIM].astype(F32).T.astype(BF16)


def _kvprep(proj, kv_norm_g, KC):
    M = proj.shape[0]
    return pl.pallas_call(
        _kvprep_kernel,
        grid=(M // KC,),
        in_specs=[pl.BlockSpec((KC, DSA_LATENT), lambda i: (i, COL_DKV // DSA_LATENT)),
                  pl.BlockSpec((KC, 128), lambda i: (i, COL_MISC // 128)),
                  pl.BlockSpec((1, DSA_LATENT), lambda i: (0, 0))],
        out_specs=[pl.BlockSpec((KC, DSA_LATENT), lambda i: (i, 0)),
                   pl.BlockSpec((1, DSA_LATENT, KC), lambda i: (i, 0, 0)),
                   pl.BlockSpec((1, IDX_DIM, KC), lambda i: (i, 0, 0))],
        out_shape=[jax.ShapeDtypeStruct((M, DSA_LATENT), BF16),
                   jax.ShapeDtypeStruct((M // KC, DSA_LATENT, KC), BF16),
                   jax.ShapeDtypeStruct((M // KC, IDX_DIM, KC), BF16)],
        compiler_params=_params(("parallel",)),
        name="kvprep",
    )(proj, proj, kv_norm_g.reshape(1, DSA_LATENT))


def _dsa_kernel(dq_ref, iq_ref, misc_ref, ikt_ref, ckv_ref, ckvt_ref, wuk_ref, wuv_ref, o_ref,
                keys_ref, qlat_ref, m_ref, l_ref, acc_ref, thr_ref, tie_ref, *, S, KC, k_sel):
    QB = Q_BLOCK
    H = DSA_HEADS
    qb = pl.program_id(1)
    nkc = (qb * QB + QB + KC - 1) // KC
    t_pos = qb * QB + lax.broadcasted_iota(I32, (QB, 1), 0)
    lane_pos = lax.broadcasted_iota(I32, (1, KC), 1)

    iw = misc_ref[:, MISC_IW:MISC_IW + IDX_HEADS].astype(F32) * (IDX_HEADS ** -0.5) * (IDX_DIM ** -0.5)
    iq = iq_ref[...]

    def score_body(kc, carry):
        k0 = pl.multiple_of(kc * KC, KC)
        ikc = ikt_ref[kc]
        score = jnp.zeros((QB, KC), F32)
        for h in range(IDX_HEADS):
            rel = _dot(iq[:, h * IDX_DIM:(h + 1) * IDX_DIM], ikc)
            rel = jnp.maximum(rel, 0.0)
            score = score + iw[:, h:h + 1] * rel
        valid = (k0 + lane_pos) <= t_pos
        keys_ref[kc] = jnp.where(valid, score, -jnp.inf)
        return carry

    lax.fori_loop(0, nkc, score_body, 0)

    def count(pred):
        def body(kc, acc):
            kv = keys_ref[kc]
            for g in range(KC // 128):
                acc = acc + jnp.where(pred(kv[:, g * 128:(g + 1) * 128], kc * KC + g * 128), 1.0, 0.0)
            return acc
        acc = lax.fori_loop(0, nkc, body, jnp.zeros((QB, 128), F32))
        return jnp.sum(acc, axis=1, keepdims=True)

    def as_float(code):
        return pltpu.bitcast(code ^ ((code >> 31) & 0x7FFFFFFF), F32)

    def bit_body(i, code):
        cand = code + (jnp.int32(1) << (31 - i))
        cand_f = as_float(cand)
        cnt = count(lambda kv, base: kv >= cand_f)
        return jnp.where(cnt >= k_sel, cand, code)

    code = lax.fori_loop(0, 32, bit_body, jnp.full((QB, 1), INT_MIN, I32))
    thr = jnp.where(code == INT_MIN, -jnp.inf, as_float(code))
    thr_ref[...] = thr
    n_gt = count(lambda kv, base: kv > thr)
    n_ge = count(lambda kv, base: kv >= thr)
    tie_ref[...] = jnp.full((QB, 1), S, I32)

    @pl.when(jnp.max(n_ge) > k_sel)
    def _():
        need = k_sel - n_gt
        lane = lax.broadcasted_iota(I32, (1, 128), 1)

        def pos_body(i, p):
            cand = p + (jnp.int32(1) << (S.bit_length() - 1 - i))
            cnt = count(lambda kv, base: (kv == thr) & ((base + lane) < cand))
            return jnp.where(cnt < need, cand, p)

        tie_ref[...] = lax.fori_loop(0, S.bit_length(), pos_body, jnp.zeros((QB, 1), I32))

    thr = thr_ref[...]
    tie = tie_ref[...]

    for h in range(H):
        ql = _dot(dq_ref[:, h * DSA_HEAD_DIM:(h + 1) * DSA_HEAD_DIM], wuk_ref[h])
        qlat_ref[h * QB:(h + 1) * QB, :] = ql.astype(BF16)
    m_ref[...] = jnp.full(m_ref.shape, MASK_NEG, F32)
    l_ref[...] = jnp.zeros(l_ref.shape, F32)
    acc_ref[...] = jnp.zeros(acc_ref.shape, F32)

    log2e = 1.4426950408889634
    c_raw = (DSA_HEAD_DIM ** -0.5) * log2e
    HG = 2
    head = lax.broadcasted_iota(I32, (H * QB, 1), 0) // QB
    slope2 = jnp.exp2((head + 1).astype(F32) * (-8.0 / H)) * log2e

    def att_body(kc, carry, final_max):
        k0 = pl.multiple_of(kc * KC, KC)
        kv = ckv_ref[pl.ds(k0, KC), :]
        keyv = keys_ref[kc]
        kpos = k0 + lane_pos
        dist = t_pos - kpos
        sel = ((keyv > thr) | ((keyv == thr) & (kpos <= tie))) & (dist >= 0)
        dmask = jnp.where(sel, dist.astype(F32), MASK_DIST)
        dmask = jnp.concatenate([dmask] * HG, axis=0)
        kvt = ckvt_ref[kc]
        for g in range(H // HG):
            rows = slice(g * HG * QB, (g + 1) * HG * QB)
            lg = _dot(qlat_ref[rows], kvt) * c_raw - slope2[rows] * dmask
            if final_max:
                p = jnp.exp2(lg - m_ref[rows])
                l_ref[rows] = l_ref[rows] + jnp.sum(p, axis=1, keepdims=True)
                acc_ref[rows] = acc_ref[rows] + _dot(p.astype(BF16), kv)
            else:
                m_ref[rows] = jnp.maximum(m_ref[rows], jnp.max(lg, axis=1, keepdims=True))
        return carry

    lax.fori_loop(0, nkc, functools.partial(att_body, final_max=False), 0)
    lax.fori_loop(0, nkc, functools.partial(att_body, final_max=True), 0)

    for h in range(H):
        rows = slice(h * QB, (h + 1) * QB)
        o_lat = acc_ref[rows] / l_ref[rows]
        o_ref[:, h * DSA_HEAD_DIM:(h + 1) * DSA_HEAD_DIM] = _dot(o_lat.astype(BF16), wuv_ref[h]).astype(BF16)


def _dsa_key_chunk(S):
    return min(1024, S)


def _dsa(proj, ikt, ckv, ckvt, w_uk, w_uv, B, S):
    M = proj.shape[0]
    QB = Q_BLOCK
    nqb = S // QB
    KC = _dsa_key_chunk(S)
    k_sel = min(DSA_TOPK, S // 4)
    HD = DSA_HEADS * DSA_HEAD_DIM
    return pl.pallas_call(
        functools.partial(_dsa_kernel, S=S, KC=KC, k_sel=k_sel),
        grid=(B, nqb),
        in_specs=[pl.BlockSpec((QB, HD), lambda b, q: (b * nqb + q, COL_DQ // HD)),
                  pl.BlockSpec((QB, IDX_HEADS * IDX_DIM), lambda b, q: (b * nqb + q, COL_IQ // (IDX_HEADS * IDX_DIM))),
                  pl.BlockSpec((QB, 128), lambda b, q: (b * nqb + q, COL_MISC // 128)),
                  pl.BlockSpec((S // KC, IDX_DIM, KC), lambda b, q: (b, 0, 0)),
                  pl.BlockSpec((S, DSA_LATENT), lambda b, q: (b, 0)),
                  pl.BlockSpec((S // KC, DSA_LATENT, KC), lambda b, q: (b, 0, 0)),
                  pl.BlockSpec((DSA_HEADS, DSA_HEAD_DIM, DSA_LATENT), lambda b, q: (0, 0, 0)),
                  pl.BlockSpec((DSA_HEADS, DSA_LATENT, DSA_HEAD_DIM), lambda b, q: (0, 0, 0))],
        out_specs=pl.BlockSpec((QB, HD), lambda b, q: (b * nqb + q, 0)),
        out_shape=jax.ShapeDtypeStruct((M, HD), BF16),
        scratch_shapes=[pltpu.VMEM((S // KC, QB, KC), F32),
                        pltpu.VMEM((DSA_HEADS * QB, DSA_LATENT), BF16),
                        pltpu.VMEM((DSA_HEADS * QB, 1), F32),
                        pltpu.VMEM((DSA_HEADS * QB, 1), F32),
                        pltpu.VMEM((DSA_HEADS * QB, DSA_LATENT), F32),
                        pltpu.VMEM((QB, 1), F32),
                        pltpu.VMEM((QB, 1), I32)],
        compiler_params=_params(("parallel", "arbitrary")),
        name="dsa",
    )(proj, proj, proj, ikt, ckv, ckvt, w_uk, w_uv)


def _merge_kernel(ya_ref, yb_ref, za_ref, zb_ref, x_ref, g1_ref, sc2_ref, sh2_ref, n2_ref,
                  wa_ref, wb_ref, wo_ref, wq_ref, k1_ref, k2_ref,
                  x1_ref, h2_ref, s1_ref, s2_ref):
    ya = _dot(ya_ref[...], wa_ref[...])
    yb = _dot(yb_ref[...], wb_ref[...])
    mix = jax.nn.sigmoid(za_ref[...].astype(F32)) * ya + jax.nn.sigmoid(zb_ref[...].astype(F32)) * yb
    y = _dot(mix.astype(BF16), wo_ref[...])
    x1 = x_ref[...] + g1_ref[0] * y
    x1_ref[...] = x1
    ms = jnp.mean(x1 * x1, axis=-1, keepdims=True)
    h2 = (x1 * lax.rsqrt(ms + NORM_EPS) * n2_ref[...]) * (1.0 + sc2_ref[0]) + sh2_ref[0]
    h2b = h2.astype(BF16)
    h2_ref[...] = h2b
    q = _dot(h2b, wq_ref[...]).astype(BF16)
    half = PEER_DKEY // 2
    for h in range(PEER_HEADS):
        s1_ref[h] = _dot_nt(k1_ref[...], q[:, h * PEER_DKEY:h * PEER_DKEY + half])
        s2_ref[h] = _dot_nt(k2_ref[...], q[:, h * PEER_DKEY + half:(h + 1) * PEER_DKEY])


def _merge(ya, yb, proj, x2, ada3, norm2_g, wa, wb, wo, wq, k1, k2, S):
    M, D = x2.shape
    tm = min(512, S)
    tpb = S // tm
    full = lambda shape: pl.BlockSpec(shape, lambda i: (0,) * len(shape))
    adaspec = lambda k: pl.BlockSpec((1, 1, D), lambda i: (i // tpb, 0, k))
    return pl.pallas_call(
        _merge_kernel,
        grid=(M // tm,),
        in_specs=[pl.BlockSpec((tm, D), lambda i: (i, 0)),
                  pl.BlockSpec((tm, D), lambda i: (i, 0)),
                  pl.BlockSpec((tm, D), lambda i: (i, COL_ZA // D)),
                  pl.BlockSpec((tm, D), lambda i: (i, COL_ZB // D)),
                  pl.BlockSpec((tm, D), lambda i: (i, 0)),
                  adaspec(2), adaspec(4), adaspec(3),
                  full((1, D)),
                  full((D, D)), full((D, D)), full((D, D)), full((D, PEER_HEADS * PEER_DKEY)),
                  full((PEER_NKEYS, PEER_DKEY // 2)), full((PEER_NKEYS, PEER_DKEY // 2))],
        out_specs=[pl.BlockSpec((tm, D), lambda i: (i, 0)),
                   pl.BlockSpec((tm, D), lambda i: (i, 0)),
                   pl.BlockSpec((PEER_HEADS, PEER_NKEYS, tm), lambda i: (0, 0, i)),
                   pl.BlockSpec((PEER_HEADS, PEER_NKEYS, tm), lambda i: (0, 0, i))],
        out_shape=[jax.ShapeDtypeStruct((M, D), F32),
                   jax.ShapeDtypeStruct((M, D), BF16),
                   jax.ShapeDtypeStruct((PEER_HEADS, PEER_NKEYS, M), F32),
                   jax.ShapeDtypeStruct((PEER_HEADS, PEER_NKEYS, M), F32)],
        compiler_params=_params(("parallel",)),
        name="merge",
    )(ya, yb, proj, proj, x2, ada3, ada3, ada3, norm2_g.reshape(1, D), wa, wb, wo, wq, k1, k2)


_PAIRS = [(j1, j2) for j1 in range(PEER_TOPK) for j2 in range(PEER_TOPK) if (j1 + 1) * (j2 + 1) <= PEER_TOPK]
_NCAND = -(-len(_PAIRS) // 8) * 8


def _extract_top(s, rounds):
    neg_iota = (-lax.broadcasted_iota(I32, s.shape, 0)).astype(F32)
    vals, idxs = [], []
    for _ in range(rounds):
        m = jnp.max(s, axis=0, keepdims=True)
        neg_idx = jnp.max(jnp.where(s == m, neg_iota, -jnp.inf), axis=0, keepdims=True)
        s = jnp.where(neg_iota == neg_idx, -jnp.inf, s)
        vals.append(m)
        idxs.append(-neg_idx)
    return vals, idxs


def _route_kernel(s1_ref, s2_ref, ia_ref, ib_ref, g_ref, ta_ref, tb_ref, tg_ref):
    R = s1_ref.shape[2]
    K = PEER_TOPK
    pad = _NCAND - len(_PAIRS)

    def head_body(h, carry):
        v1, i1 = _extract_top(s1_ref[h], K)
        v2, i2 = _extract_top(s2_ref[h], K)
        cand = jnp.concatenate([v1[a] + v2[b] for a, b in _PAIRS] + [jnp.full((pad, R), -jnp.inf, F32)], axis=0)
        ca = jnp.concatenate([i1[a] for a, b in _PAIRS] + [jnp.zeros((pad, R), F32)], axis=0)
        cb = jnp.concatenate([i2[b] for a, b in _PAIRS] + [jnp.zeros((pad, R), F32)], axis=0)
        best, pos = _extract_top(cand, K)
        iota = lax.broadcasted_iota(I32, cand.shape, 0).astype(F32)
        e = [jnp.exp(bv - best[0]) for bv in best]
        denom = e[0]
        for j in range(1, K):
            denom = denom + e[j]
        ia_rows, ib_rows, g_rows = [], [], []
        for j in range(K):
            hit = iota == pos[j]
            ia_rows.append(jnp.sum(jnp.where(hit, ca, 0.0), axis=0, keepdims=True))
            ib_rows.append(jnp.sum(jnp.where(hit, cb, 0.0), axis=0, keepdims=True))
            g_rows.append(e[j] / denom)
        rows = pl.ds(pl.multiple_of(h * K, K), K)
        ta_ref[rows, :] = jnp.concatenate(ia_rows, axis=0)
        tb_ref[rows, :] = jnp.concatenate(ib_rows, axis=0)
        tg_ref[rows, :] = jnp.concatenate(g_rows, axis=0)
        return carry

    lax.fori_loop(0, PEER_HEADS, head_body, 0)
    ia_ref[...] = ta_ref[...].T
    ib_ref[...] = tb_ref[...].T
    g_ref[...] = tg_ref[...].T


def _route(s1, s2):
    M = s1.shape[2]
    R = 512
    NS = PEER_HEADS * PEER_TOPK
    spec_in = pl.BlockSpec((PEER_HEADS, PEER_NKEYS, R), lambda i: (0, 0, i))
    spec_out = pl.BlockSpec((R, NS), lambda i: (i, 0))
    return pl.pallas_call(
        _route_kernel,
        grid=(M // R,),
        in_specs=[spec_in, spec_in],
        out_specs=[spec_out, spec_out, spec_out],
        out_shape=[jax.ShapeDtypeStruct((M, NS), F32)] * 3,
        scratch_shapes=[pltpu.VMEM((NS, R), F32)] * 3,
        compiler_params=_params(("parallel",)),
        name="route",
    )(s1, s2)


GM_PITCH = PEER_NKEYS // 2 + 4

def _peer_kernel(h2_ref, ia_ref, ib_ref, g_ref, u_ref, v_ref, x1_ref, g2_ref, fg_ref, o_ref,
                 gm_ref, acc_ref, *, T, AE, final_norm):
    e = pl.program_id(1)
    NK = PEER_NKEYS
    NA = NK // 2
    steps_per_build = NA // AE
    TOK_UNROLL = 128

    @pl.when(e == 0)
    def _():
        acc_ref[...] = jnp.zeros_like(acc_ref)

    @pl.when(e % steps_per_build == 0)
    def _():
        a_base = (e // steps_per_build) * NA
        sub_a = (lax.broadcasted_iota(I32, (NA, NK), 0) + a_base).astype(F32)
        sub_b = lax.broadcasted_iota(I32, (NK, NK), 0).astype(F32)

        zero = jnp.zeros((NA, NK), BF16)

        def tok_body(t8, carry):
            r0 = pl.multiple_of(t8 * TOK_UNROLL, TOK_UNROLL)
            base = pl.multiple_of(r0 * GM_PITCH, TOK_UNROLL)
            ia8 = ia_ref[pl.ds(r0, TOK_UNROLL), :]
            ib8 = ib_ref[pl.ds(r0, TOK_UNROLL), :]
            g8 = g_ref[pl.ds(r0, TOK_UNROLL), :]
            for j in range(0, TOK_UNROLL, 2):
                am = [jnp.where(sub_a == ia8[k:k + 1, :], g8[k:k + 1, :], 0.0).astype(BF16) for k in (j, j + 1)]
                bm = [jnp.where(sub_b == ib8[k:k + 1, :], 1.0, 0.0).astype(BF16) for k in (j, j + 1)]
                lhs = jnp.concatenate([jnp.concatenate([am[0], zero], axis=1),
                                       jnp.concatenate([zero, am[1]], axis=1)], axis=0)
                pair = _dot_nt(lhs, jnp.concatenate(bm, axis=1))
                gm_ref[pl.ds(base + j * GM_PITCH, NA), :] = pair[:NA]
                gm_ref[pl.ds(base + (j + 1) * GM_PITCH, NA), :] = pair[NA:]
            return carry

        lax.fori_loop(0, T // TOK_UNROLL, tok_body, 0)

    a_loc = (e % steps_per_build) * AE
    s = _dot_nt(h2_ref[...], u_ref[...])
    gt = jnp.concatenate([gm_ref[pl.ds(a_loc + j, T, stride=GM_PITCH), :] for j in range(AE)], axis=1)
    act = 0.5 * s * (1.0 + lax.erf(s * (0.5 ** 0.5)))
    acc_ref[...] += _dot((act * gt).astype(BF16), v_ref[...])

    @pl.when(e == pl.num_programs(1) - 1)
    def _():
        x2 = x1_ref[...] + g2_ref[0] * acc_ref[...]
        if final_norm:
            ms = jnp.mean(x2 * x2, axis=-1, keepdims=True)
            x2 = x2 * lax.rsqrt(ms + NORM_EPS) * fg_ref[...]
        o_ref[...] = x2


def _peer(h2, ia, ib, g, u, v, x1, ada3, final_g, S, final_norm):
    M, D = x1.shape
    T = min(512, S)
    AE = 8
    ET = AE * PEER_NKEYS
    NE = u.shape[0]
    NS = PEER_HEADS * PEER_TOPK
    tpb = S // T
    return pl.pallas_call(
        functools.partial(_peer_kernel, T=T, AE=AE, final_norm=final_norm),
        grid=(M // T, NE // ET),
        in_specs=[pl.BlockSpec((T, D), lambda i, e: (i, 0)),
                  pl.BlockSpec((T, NS), lambda i, e: (i, 0)),
                  pl.BlockSpec((T, NS), lambda i, e: (i, 0)),
                  pl.BlockSpec((T, NS), lambda i, e: (i, 0)),
                  pl.BlockSpec((ET, D), lambda i, e: (e, 0)),
                  pl.BlockSpec((ET, D), lambda i, e: (e, 0)),
                  pl.BlockSpec((T, D), lambda i, e: (i, 0)),
                  pl.BlockSpec((1, 1, D), lambda i, e: (i // tpb, 0, 5)),
                  pl.BlockSpec((1, D), lambda i, e: (0, 0))],
        out_specs=pl.BlockSpec((T, D), lambda i, e: (i, 0)),
        out_shape=jax.ShapeDtypeStruct((M, D), F32),
        scratch_shapes=[pltpu.VMEM((T * GM_PITCH, PEER_NKEYS), F32),
                        pltpu.VMEM((T, D), F32)],
        compiler_params=_params(("parallel", "arbitrary")),
        name="peer",
    )(h2, ia, ib, g, u, v, x1, ada3, final_g.reshape(1, D))


def _cat_w_in(w_in):
    sizes = (GLA_DK, GLA_DK, GLA_DV, GLA_DV, GLA_GATE_RANK, DSA_HEADS * DSA_HEAD_DIM, DSA_LATENT,
             IDX_HEADS * IDX_DIM, IDX_DIM, IDX_HEADS, D_MODEL, D_MODEL)
    offs = [0]
    for s in sizes:
        offs.append(offs[-1] + s)
    gq, gk, gv, gr, glow, dq, dkv, iq, ik, iw, za, zb = [w_in[:, offs[i]:offs[i + 1]] for i in range(len(sizes))]
    pad_misc = jnp.zeros((w_in.shape[0], 128 - (GLA_GATE_RANK + IDX_DIM + IDX_HEADS)), w_in.dtype)
    pad_end = jnp.zeros((w_in.shape[0], PROJ_COLS - (COL_MISC + 128)), w_in.dtype)
    return jnp.concatenate([gv, gr, dq, za, zb, gq, gk, iq, dkv, glow, ik, iw, pad_misc, pad_end], axis=1).astype(BF16)


def kernel(x, c, w_ada, b_ada, norm1_g, w_in, gla_w_gate_up, gla_b_gate, gla_norm_g, dsa_kv_norm_g, dsa_w_uk,
           dsa_w_uv, w_branch_a, w_branch_b, w_out, norm2_g, peer_w_q, peer_sub_keys_1, peer_sub_keys_2,
           peer_u, peer_v, final_norm_g):
    B, S, D = x.shape
    depth = w_ada.shape[0]
    x2 = x.reshape(B * S, D)
    for l in range(depth):
        ada3 = _ada(c, w_ada[l], b_ada[l]).reshape(B, 1, 6 * D)
        proj = _proj(x2, ada3, norm1_g[l], _cat_w_in(w_in[l]), S)
        ya = _gla(proj, gla_w_gate_up[l], gla_b_gate[l], gla_norm_g[l], B, S)
        ckv, ckvt, ikt = _kvprep(proj, dsa_kv_norm_g[l], _dsa_key_chunk(S))
        yb = _dsa(proj, ikt, ckv, ckvt, dsa_w_uk[l].astype(BF16), dsa_w_uv[l].astype(BF16), B, S)
        x1, h2, s1, s2 = _merge(ya, yb, proj, x2, ada3, norm2_g[l],
                                w_branch_a[l].astype(BF16), w_branch_b[l].astype(BF16), w_out[l].astype(BF16),
                                peer_w_q[l].astype(BF16), peer_sub_keys_1[l].astype(BF16),
                                peer_sub_keys_2[l].astype(BF16), S)
        ia, ib, g = _route(s1, s2)
        x2 = _peer(h2, ia, ib, g, peer_u[l].astype(BF16), peer_v[l].astype(BF16), x1, ada3, final_norm_g, S,
                   final_norm=(l == depth - 1))
    return x2.reshape(B, S, D)
```
